```python
import math
import jax, jax.numpy as jnp
from jax import lax
import numpy as np

D_MODEL = 2048
BATCH = 2
SEQ = 4096
DEPTH = 1
DEC_BATCH = 32
DEC_SEQ = 64
PAST_LEN = 2048

CHUNK = 64
H_A = 8
DK_A = 64
DV_A = 2 * DK_A
W_QK_A = H_A * 2 * DK_A
W_A = H_A * DV_A
H_B = 8
DH_B = 128
W_B = H_B * DH_B
N_PREV = 8
BAND = N_PREV + 1
BAND_ROWS = N_PREV * CHUNK
MAX_REL = 128
FFN_DIM = 4 * D_MODEL
ROPE_THETA = 10000.0
EPS = 1e-6
Q_BLOCK = 128
NEG = -1e30

kernel_name = 'hybrid_streaming_encoder_step'


def rms_norm(x, g):
    xf = x.astype(jnp.float32)
    y = xf * lax.rsqrt(jnp.mean(xf * xf, axis=-1, keepdims=True) + EPS)
    return (y * g.astype(jnp.float32)).astype(x.dtype)


def swiglu_ffn(x, g, w_gate, w_up, w_down):
    h = rms_norm(x, g)
    return (jax.nn.silu(h @ w_gate) * (h @ w_up)) @ w_down


def rope(x, pos):
    half = x.shape[-1] // 2
    inv = ROPE_THETA ** (-jnp.arange(half, dtype=jnp.float32) / half)
    ang = pos.astype(jnp.float32)[:, None] * inv[None, :]
    cos = jnp.cos(ang)[:, None, :]
    sin = jnp.sin(ang)[:, None, :]
    x1 = x[..., :half].astype(jnp.float32)
    x2 = x[..., half:].astype(jnp.float32)
    out = jnp.concatenate([x1 * cos - x2 * sin, x2 * cos + x1 * sin], axis=-1)
    return out.astype(x.dtype)


def split_projection(h, w_in, pos):
    b, s, _ = h.shape
    z = h @ w_in
    cuts = [W_QK_A, 2 * W_QK_A, 2 * W_QK_A + W_A, 2 * W_QK_A + W_A + W_B,
            2 * W_QK_A + W_A + 2 * W_B, 2 * W_QK_A + W_A + 3 * W_B,
            2 * W_QK_A + W_A + 3 * W_B + D_MODEL]
    qa, ka, va, qb, kb, vb, ga, gb = jnp.split(z, cuts, axis=-1)
    qa = rope(qa.reshape(b, s, 2 * H_A, DK_A), pos).reshape(b, s, H_A, 2, DK_A)
    ka = rope(ka.reshape(b, s, 2 * H_A, DK_A), pos).reshape(b, s, H_A, 2 * DK_A)
    va = va.reshape(b, s, H_A, DV_A)
    qb = qb.reshape(b, s, H_B, DH_B)
    kb = kb.reshape(b, s, H_B, DH_B)
    vb = vb.reshape(b, s, H_B, DH_B)
    return qa, ka, va, qb, kb, vb, ga, gb


def diff_attn_core(q, k, v, mask, lam):
    k = k.reshape(k.shape[:3] + (2, DK_A))
    s = jnp.einsum('bqhmd,bkhmd->bhmqk', q, k).astype(jnp.float32) * (DK_A ** -0.5)
    if mask is not None:
        s = jnp.where(mask, s, NEG)
    p = jax.nn.softmax(s, axis=-1)
    a = p[:, :, 0] - lam * p[:, :, 1]
    return jnp.einsum('bhqk,bkhd->bqhd', a.astype(v.dtype), v)


def diff_attn_prompt(q, k, v, lam):
    b, s = q.shape[:2]
    nb = s // Q_BLOCK
    qb = q.reshape(b, nb, Q_BLOCK, H_A, 2, DK_A).swapaxes(0, 1)
    key_chunk = jnp.arange(s) // CHUNK

    def block(args):
        qi, i = args
        q_chunk = (i * Q_BLOCK + jnp.arange(Q_BLOCK)) // CHUNK
        mask = key_chunk[None, :] <= q_chunk[:, None]
        return diff_attn_core(qi, k, v, mask, lam)

    o = lax.map(block, (qb, jnp.arange(nb)))
    return o.swapaxes(0, 1).reshape(b, s, H_A, DV_A)


def rel_bias_lookup(table, q_pos, k_pos):
    rel = jnp.clip(q_pos[:, None] - k_pos[None, :], -MAX_REL, MAX_REL) + MAX_REL
    return table[:, rel].astype(jnp.float32)


def band_attn_prompt(q, k, v, table):
    b, s = q.shape[:2]
    nc = s // CHUNK
    pad = jnp.zeros((b, N_PREV, CHUNK, H_B, DH_B), k.dtype)
    kp = jnp.concatenate([pad, k.reshape(b, nc, CHUNK, H_B, DH_B)], axis=1)
    vp = jnp.concatenate([pad, v.reshape(b, nc, CHUNK, H_B, DH_B)], axis=1)
    idx = jnp.arange(nc)[:, None] + jnp.arange(BAND)[None, :]
    kband = kp[:, idx].reshape(b, nc, BAND * CHUNK, H_B, DH_B)
    vband = vp[:, idx].reshape(b, nc, BAND * CHUNK, H_B, DH_B)
    valid = jnp.repeat(idx >= N_PREV, CHUNK, axis=1)
    bias = rel_bias_lookup(table, BAND_ROWS + jnp.arange(CHUNK), jnp.arange(BAND * CHUNK))
    qc = q.reshape(b, nc, CHUNK, H_B, DH_B)
    sc = jnp.einsum('bcqhd,bckhd->bchqk', qc, kband).astype(jnp.float32) * (DH_B ** -0.5)
    sc = jnp.where(valid[None, :, None, None, :], sc + bias[None, None], NEG)
    p = jax.nn.softmax(sc, axis=-1)
    o = jnp.einsum('bchqk,bckhd->bcqhd', p.astype(v.dtype), vband)
    return o.reshape(b, s, H_B, DH_B)


def band_attn_sample(q, k_all, v_all, table, past_rows):
    dec = q.shape[1]
    bias = rel_bias_lookup(table, past_rows + jnp.arange(dec), jnp.arange(past_rows + dec))
    sc = jnp.einsum('bqhd,bkhd->bhqk', q, k_all).astype(jnp.float32) * (DH_B ** -0.5)
    p = jax.nn.softmax(sc + bias[None], axis=-1)
    return jnp.einsum('bhqk,bkhd->bqhd', p.astype(v_all.dtype), v_all)


def merge_branches(oa, ob, ga, gb, lam_init, subln, w_branch_a, w_branch_b, w_out):
    b, s = oa.shape[:2]
    oa = (rms_norm(oa, subln) * (1.0 - lam_init)).reshape(b, s, W_A)
    ob = ob.reshape(b, s, W_B)
    merged = jax.nn.sigmoid(ga) * (oa @ w_branch_a) + jax.nn.sigmoid(gb) * (ob @ w_branch_b)
    return merged @ w_out


def setup_inputs(seed: int = 0) -> dict:
    key = jax.random.key(seed)
    ks = jax.random.split(key, 26)
    f32 = jnp.float32

    def w(k, shape, fan_in):
        return jax.random.normal(k, shape, f32) * (fan_in ** -0.5)

    def gain(k, shape):
        return 1.0 + 0.01 * jax.random.normal(k, shape, f32)

    band_cache = min(BAND_ROWS, PAST_LEN)
    in_cols = 2 * W_QK_A + W_A + 3 * W_B + 2 * D_MODEL
    return {
        'x_prompt': jax.random.normal(ks[0], (BATCH, SEQ, D_MODEL), f32),
        'x_sample': jax.random.normal(ks[1], (DEC_BATCH, DEC_SEQ, D_MODEL), f32),
        'cache_a_k': jax.random.normal(ks[2], (DEPTH, DEC_BATCH, PAST_LEN, H_A, 2 * DK_A), f32),
        'cache_a_v': jax.random.normal(ks[3], (DEPTH, DEC_BATCH, PAST_LEN, H_A, DV_A), f32),
        'cache_b_k': jax.random.normal(ks[4], (DEPTH, DEC_BATCH, band_cache, H_B, DH_B), f32),
        'cache_b_v': jax.random.normal(ks[5], (DEPTH, DEC_BATCH, band_cache, H_B, DH_B), f32),
        'ffn1_norm': gain(ks[6], (DEPTH, D_MODEL)),
        'ffn1_w_gate': w(ks[7], (DEPTH, D_MODEL, FFN_DIM), D_MODEL),
        'ffn1_w_up': w(ks[8], (DEPTH, D_MODEL, FFN_DIM), D_MODEL),
        'ffn1_w_down': w(ks[9], (DEPTH, FFN_DIM, D_MODEL), FFN_DIM),
        'mix_norm': gain(ks[10], (DEPTH, D_MODEL)),
        'w_in': w(ks[11], (DEPTH, D_MODEL, in_cols), D_MODEL),
        'lambda_q1': 0.1 * jax.random.normal(ks[12], (DEPTH, DK_A), f32),
        'lambda_k1': 0.1 * jax.random.normal(ks[13], (DEPTH, DK_A), f32),
        'lambda_q2': 0.1 * jax.random.normal(ks[14], (DEPTH, DK_A), f32),
        'lambda_k2': 0.1 * jax.random.normal(ks[15], (DEPTH, DK_A), f32),
        'subln_a': gain(ks[16], (DEPTH, DV_A)),
        'rel_bias_b': 0.2 * jax.random.normal(ks[17], (DEPTH, H_B, 2 * MAX_REL + 1), f32),
        'w_branch_a': w(ks[18], (DEPTH, W_A, D_MODEL), W_A),
        'w_branch_b': w(ks[19], (DEPTH, W_B, D_MODEL), W_B),
        'w_out': w(ks[20], (DEPTH, D_MODEL, D_MODEL), D_MODEL),
        'ffn2_norm': gain(ks[21], (DEPTH, D_MODEL)),
        'ffn2_w_gate': w(ks[22], (DEPTH, D_MODEL, FFN_DIM), D_MODEL),
        'ffn2_w_up': w(ks[23], (DEPTH, D_MODEL, FFN_DIM), D_MODEL),
        'ffn2_w_down': w(ks[24], (DEPTH, FFN_DIM, D_MODEL), FFN_DIM),
        'final_norm': gain(ks[25], (D_MODEL,)),
    }


def reference(x_prompt, x_sample, cache_a_k, cache_a_v, cache_b_k, cache_b_v,
              ffn1_norm, ffn1_w_gate, ffn1_w_up, ffn1_w_down,
              mix_norm, w_in, lambda_q1, lambda_k1, lambda_q2, lambda_k2, subln_a,
              rel_bias_b, w_branch_a, w_branch_b, w_out,
              ffn2_norm, ffn2_w_gate, ffn2_w_up, ffn2_w_down, final_norm):
    seq = x_prompt.shape[1]
    dec_seq = x_sample.shape[1]
    past_a = cache_a_k.shape[2]
    past_b = cache_b_k.shape[2]
    prompt_band_rows = min(BAND_ROWS, seq)
    pos_prompt = jnp.arange(seq)
    pos_sample = past_a + jnp.arange(dec_seq)
    f32 = jnp.float32

    xp, xs = x_prompt, x_sample
    ak_p, av_p, bk_p, bv_p = [], [], [], []
    ak_s, av_s, bk_s, bv_s = [], [], [], []
    for layer in range(DEPTH):
        lam_init = 0.8 - 0.6 * math.exp(-0.3 * layer)
        lam = (jnp.exp(jnp.sum(lambda_q1[layer].astype(f32) * lambda_k1[layer].astype(f32)))
               - jnp.exp(jnp.sum(lambda_q2[layer].astype(f32) * lambda_k2[layer].astype(f32)))
               + lam_init)
        ffn1 = (ffn1_norm[layer], ffn1_w_gate[layer], ffn1_w_up[layer], ffn1_w_down[layer])
        ffn2 = (ffn2_norm[layer], ffn2_w_gate[layer], ffn2_w_up[layer], ffn2_w_down[layer])
        merge_w = (subln_a[layer], w_branch_a[layer], w_branch_b[layer], w_out[layer])

        xp = xp + 0.5 * swiglu_ffn(xp, *ffn1)
        qa, ka, va, qb, kb, vb, ga, gb = split_projection(rms_norm(xp, mix_norm[layer]), w_in[layer], pos_prompt)
        oa = diff_attn_prompt(qa, ka, va, lam)
        ob = band_attn_prompt(qb, kb, vb, rel_bias_b[layer])
        xp = xp + merge_branches(oa, ob, ga, gb, lam_init, *merge_w)
        xp = xp + 0.5 * swiglu_ffn(xp, *ffn2)
        ak_p.append(ka)
        av_p.append(va)
        bk_p.append(kb[:, seq - prompt_band_rows:])
        bv_p.append(vb[:, seq - prompt_band_rows:])

        xs = xs + 0.5 * swiglu_ffn(xs, *ffn1)
        qa, ka, va, qb, kb, vb, ga, gb = split_projection(rms_norm(xs, mix_norm[layer]), w_in[layer], pos_sample)
        ka_all = jnp.concatenate([cache_a_k[layer], ka], axis=1)
        va_all = jnp.concatenate([cache_a_v[layer], va], axis=1)
        oa = diff_attn_core(qa, ka_all, va_all, None, lam)
        kb_all = jnp.concatenate([cache_b_k[layer], kb], axis=1)
        vb_all = jnp.concatenate([cache_b_v[layer], vb], axis=1)
        ob = band_attn_sample(qb, kb_all, vb_all, rel_bias_b[layer], past_b)
        xs = xs + merge_branches(oa, ob, ga, gb, lam_init, *merge_w)
        xs = xs + 0.5 * swiglu_ffn(xs, *ffn2)
        ak_s.append(ka)
        av_s.append(va)
        bk_s.append(kb_all[:, dec_seq:])
        bv_s.append(vb_all[:, dec_seq:])

    y_prompt = rms_norm(xp, final_norm)
    y_sample = rms_norm(xs, final_norm)
    return (y_prompt, y_sample,
            jnp.stack(ak_p), jnp.stack(av_p), jnp.stack(bk_p), jnp.stack(bv_p),
            jnp.stack(ak_s), jnp.stack(av_s), jnp.stack(bk_s), jnp.stack(bv_s))
```

```python
import functools
import math

import jax
import jax.numpy as jnp
from jax import lax
from jax.experimental import pallas as pl
from jax.experimental.pallas import tpu as pltpu

F32 = jnp.float32
BF16 = jnp.bfloat16

D_MODEL = 2048
CHUNK = 64
H_A = 8
DK_A = 64
DV_A = 2 * DK_A
W_QK_A = H_A * 2 * DK_A
W_A = H_A * DV_A
H_B = 8
DH_B = 128
W_B = H_B * DH_B
N_PREV = 8
BAND_ROWS = N_PREV * CHUNK
MAX_REL = 128
FFN_DIM = 4 * D_MODEL
ROPE_THETA = 10000.0
EPS = 1e-6
NEG = -1e30

HEAD_LANES = 128
VMEM_LIMIT = 52 * 1024 * 1024

FFN_BM = 512
FFN_BF = 512
PROJ_BM = 512
PROJ_BN = 1024
MERGE_BM = 256
ATT_BQ = 256
BAND_WIN = BAND_ROWS + ATT_BQ


def _params(*sem):
    return pltpu.CompilerParams(dimension_semantics=sem, vmem_limit_bytes=VMEM_LIMIT)


def _rms(x, g):
    return x * lax.rsqrt(jnp.mean(x * x, axis=-1, keepdims=True) + EPS) * g


def _ffn_kernel(x_ref, g_ref, wg_ref, wu_ref, wd_ref, g2_ref, *rest, mode):
    if mode == "mix":
        o_ref, hn_ref, h_ref = rest
    else:
        o_ref, h_ref = rest
    f = pl.program_id(1)

    @pl.when(f == 0)
    def _():
        x = x_ref[...]
        h_ref[...] = _rms(x, g_ref[...]).astype(BF16)
        o_ref[...] = x

    h = h_ref[...]
    a = jnp.dot(h, wg_ref[...], preferred_element_type=F32)
    b = jnp.dot(h, wu_ref[...], preferred_element_type=F32)
    t = (0.5 * (a * jax.nn.sigmoid(a)) * b).astype(BF16)
    o_ref[...] += jnp.dot(t, wd_ref[...], preferred_element_type=F32)

    @pl.when(f == pl.num_programs(1) - 1)
    def _():
        y = _rms(o_ref[...], g2_ref[...])
        if mode == "mix":
            hn_ref[...] = y.astype(BF16)
        else:
            o_ref[...] = y


def _ffn(x, g, wg, wu, wd, g2, mode):
    m = x.shape[0]
    bm, bf = FFN_BM, FFN_BF
    grid = (m // bm, FFN_DIM // bf)
    row = pl.BlockSpec((bm, D_MODEL), lambda i, f: (i, 0))
    vec = pl.BlockSpec((1, D_MODEL), lambda i, f: (0, 0))
    in_specs = [
        row, vec,
        pl.BlockSpec((D_MODEL, bf), lambda i, f: (0, f)),
        pl.BlockSpec((D_MODEL, bf), lambda i, f: (0, f)),
        pl.BlockSpec((bf, D_MODEL), lambda i, f: (f, 0)),
        vec,
    ]
    if mode == "mix":
        out_shape = (jax.ShapeDtypeStruct((m, D_MODEL), F32), jax.ShapeDtypeStruct((m, D_MODEL), BF16))
        out_specs = (row, row)
    else:
        out_shape = jax.ShapeDtypeStruct((m, D_MODEL), F32)
        out_specs = row
    return pl.pallas_call(
        functools.partial(_ffn_kernel, mode=mode),
        grid=grid, in_specs=in_specs, out_specs=out_specs, out_shape=out_shape,
        scratch_shapes=[pltpu.VMEM((bm, D_MODEL), BF16)],
        compiler_params=_params("parallel", "arbitrary"),
        name="ffn_" + mode,
    )(x, g, wg, wu, wd, g2)


def _rope_slabs(z, cos, sin_signed):
    lane = lax.broadcasted_iota(jnp.int32, cos.shape, 1)
    first_half = (lane % DK_A) < (DK_A // 2)
    outs = []
    for s in range(z.shape[1] // HEAD_LANES):
        slab = z[:, s * HEAD_LANES:(s + 1) * HEAD_LANES]
        partner = jnp.where(first_half,
                            pltpu.roll(slab, HEAD_LANES - DK_A // 2, 1),
                            pltpu.roll(slab, DK_A // 2, 1))
        outs.append(slab * cos + partner * sin_signed)
    return jnp.concatenate(outs, axis=1)


def _proj_kernel(h_ref, w_ref, *rest, epilogue):
    rope = epilogue in ("rope_q", "rope_kv")
    heads = epilogue in ("rope_kv", "kv")
    if rope:
        cos_ref, sin_ref, *rest = rest
    z = jnp.dot(h_ref[...], w_ref[...], preferred_element_type=F32)
    if rope:
        z = _rope_slabs(z, cos_ref[...], sin_ref[...])
    if epilogue == "rope_q":
        z = z * (DK_A ** -0.5)
    elif epilogue == "sigmoid":
        z = jax.nn.sigmoid(z)
    if heads:
        o_ref, o16_ref = rest
        for h in range(z.shape[1] // HEAD_LANES):
            zh = z[:, h * HEAD_LANES:(h + 1) * HEAD_LANES]
            o_ref[:, h, :] = zh
            o16_ref[h] = zh.astype(BF16)
    else:
        (o_ref,) = rest
        o_ref[...] = z.astype(o_ref.dtype)


def _proj(hn, w_in, col0, ncols, epilogue, rope_tabs=None, rope_period=None):
    m = hn.shape[0]
    bm, bn = PROJ_BM, PROJ_BN
    joff = col0 // bn
    in_specs = [
        pl.BlockSpec((bm, D_MODEL), lambda i, j: (i, 0)),
        pl.BlockSpec((D_MODEL, bn), lambda i, j: (0, j + joff)),
    ]
    args = [hn, w_in]
    if rope_tabs is not None:
        nper = rope_period // bm
        tab = pl.BlockSpec((bm, HEAD_LANES), lambda i, j: (i % nper, 0))
        in_specs += [tab, tab]
        args += list(rope_tabs)
    if epilogue in ("rope_kv", "kv"):
        assert ncols == bn
        nh = bn // HEAD_LANES
        out_specs = (pl.BlockSpec((bm, nh, HEAD_LANES), lambda i, j: (i, 0, 0)),
                     pl.BlockSpec((nh, bm, HEAD_LANES), lambda i, j: (0, i, 0)))
        out_shape = (jax.ShapeDtypeStruct((m, nh, HEAD_LANES), F32),
                     jax.ShapeDtypeStruct((nh, m, HEAD_LANES), BF16))
    else:
        out_specs = pl.BlockSpec((bm, bn), lambda i, j: (i, j))
        out_shape = jax.ShapeDtypeStruct((m, ncols), BF16)
    return pl.pallas_call(
        functools.partial(_proj_kernel, epilogue=epilogue),
        grid=(m // bm, ncols // bn), in_specs=in_specs,
        out_specs=out_specs, out_shape=out_shape,
        compiler_params=_params("parallel", "arbitrary"),
        name="proj_" + epilogue,
    )(*args)


def _block_diag_q(q):
    lane = lax.broadcasted_iota(jnp.int32, q.shape, 1)
    zero = jnp.zeros_like(q)
    return jnp.concatenate([jnp.where(lane < DK_A, q, zero), jnp.where(lane >= DK_A, q, zero)], axis=0)


def _qk(q, k):
    return lax.dot_general(q, k, (((1,), (1,)), ((), ())), preferred_element_type=F32)


def _diff_finish(acc, l, lam, subln, n):
    o = acc / l
    o = o[:n] - lam * o[n:]
    return _rms(o, subln)


def _attn_a_prompt_kernel(lam_ref, q_ref, kb_ref, vb_ref, subln_ref, o_ref, *, lam_scale):
    qi = pl.program_id(2)
    bq = q_ref.shape[0]
    qbd = _block_diag_q(q_ref[...])

    def step(j, carry, masked):
        m, l, acc = carry
        start = pl.multiple_of(j * bq, bq)
        ks = kb_ref[pl.ds(start, bq), :]
        vs = vb_ref[pl.ds(start, bq), :]
        s = _qk(qbd, ks)
        if masked:
            r = lax.broadcasted_iota(jnp.int32, s.shape, 0) % bq
            c = lax.broadcasted_iota(jnp.int32, s.shape, 1)
            s = jnp.where((c // CHUNK) <= (r // CHUNK), s, NEG)
        m_new = jnp.maximum(m, jnp.max(s, axis=-1, keepdims=True))
        alpha = jnp.exp(m - m_new)
        p = jnp.exp(s - m_new)
        l = alpha * l + jnp.sum(p, axis=-1, keepdims=True)
        acc = alpha * acc + jnp.dot(p.astype(BF16), vs, preferred_element_type=F32)
        return m_new, l, acc

    init = (jnp.full((2 * bq, 1), -jnp.inf, F32), jnp.zeros((2 * bq, 1), F32),
            jnp.zeros((2 * bq, HEAD_LANES), F32))
    carry = lax.fori_loop(0, qi, functools.partial(step, masked=False), init)
    _, l, acc = step(qi, carry, masked=True)
    o = _diff_finish(acc, l, lam_ref[0], subln_ref[...], bq) * lam_scale
    o_ref[...] = o.astype(o_ref.dtype)


def _attn_a_prompt(lam, qa, k16, v16, subln, batch, seq, lam_scale):
    bq = ATT_BQ
    nq = seq // bq
    kv = pl.BlockSpec((None, seq, HEAD_LANES), lambda b, h, i: (h, b, 0))
    return pl.pallas_call(
        functools.partial(_attn_a_prompt_kernel, lam_scale=lam_scale),
        grid=(batch, H_A, nq),
        in_specs=[
            pl.BlockSpec(memory_space=pltpu.SMEM),
            pl.BlockSpec((bq, HEAD_LANES), lambda b, h, i: (b * nq + i, h)),
            kv, kv,
            pl.BlockSpec((1, DV_A), lambda b, h, i: (0, 0)),
        ],
        out_specs=pl.BlockSpec((bq, HEAD_LANES), lambda b, h, i: (b * nq + i, h)),
        out_shape=jax.ShapeDtypeStruct((batch * seq, W_A), BF16),
        compiler_params=_params("parallel", "parallel", "arbitrary"),
        name="attn_a_prompt",
    )(lam, qa, k16, v16, subln)


def _softmax_two_parts(sc, sn, vc, vn):
    m = jnp.maximum(jnp.max(sc, axis=-1, keepdims=True), jnp.max(sn, axis=-1, keepdims=True))
    pc = jnp.exp(sc - m)
    pn = jnp.exp(sn - m)
    l = jnp.sum(pc, axis=-1, keepdims=True) + jnp.sum(pn, axis=-1, keepdims=True)
    acc = (jnp.dot(pc.astype(BF16), vc, preferred_element_type=F32)
           + jnp.dot(pn.astype(BF16), vn, preferred_element_type=F32))
    return acc, l


def _attn_a_sample_kernel(lam_ref, q_ref, kc_ref, vc_ref, kn_ref, vn_ref, subln_ref, o_ref, *, lam_scale):
    n = q_ref.shape[0]
    for h in range(H_A):
        cols = slice(h * HEAD_LANES, (h + 1) * HEAD_LANES)
        qbd = _block_diag_q(q_ref[:, cols])
        sc = _qk(qbd, kc_ref[:, h, :].astype(BF16))
        sn = _qk(qbd, kn_ref[h])
        acc, l = _softmax_two_parts(sc, sn, vc_ref[:, h, :].astype(BF16), vn_ref[h])
        o = _diff_finish(acc, l, lam_ref[0], subln_ref[...], n) * lam_scale
        o_ref[:, cols] = o.astype(o_ref.dtype)


def _attn_a_sample(lam, qa, cache_k, cache_v, k16, v16, subln, layer, lam_scale):
    _, nb, past, _, _ = cache_k.shape
    n = qa.shape[0] // nb
    rows = pl.BlockSpec((n, W_A), lambda b: (b, 0))
    new = pl.BlockSpec((H_A, n, HEAD_LANES), lambda b: (0, b, 0))
    old = pl.BlockSpec((None, None, past, H_A, HEAD_LANES), lambda b: (layer, b, 0, 0, 0))
    return pl.pallas_call(
        functools.partial(_attn_a_sample_kernel, lam_scale=lam_scale),
        grid=(nb,),
        in_specs=[pl.BlockSpec(memory_space=pltpu.SMEM), rows, old, old, new, new,
                  pl.BlockSpec((1, DV_A), lambda b: (0, 0))],
        out_specs=rows,
        out_shape=jax.ShapeDtypeStruct(qa.shape, BF16),
        compiler_params=_params("parallel"),
        name="attn_a_sample",
    )(lam, qa, cache_k, cache_v, k16, v16, subln)


def _attn_b_prompt_kernel(q_ref, k_ref, v_ref, bias_ref, o_ref, kb_ref, vb_ref):
    qi = pl.program_id(2)
    bq = q_ref.shape[0]
    seq = k_ref.shape[0]

    @pl.when(qi == 0)
    def _():
        zeros = jnp.zeros((BAND_ROWS, HEAD_LANES), BF16)
        kb_ref[0:BAND_ROWS, :] = zeros
        vb_ref[0:BAND_ROWS, :] = zeros
        kb_ref[BAND_ROWS:BAND_ROWS + seq, :] = k_ref[...]
        vb_ref[BAND_ROWS:BAND_ROWS + seq, :] = v_ref[...]

    start = pl.multiple_of(qi * bq, bq)
    kw = kb_ref[pl.ds(start, BAND_WIN), :]
    vw = vb_ref[pl.ds(start, BAND_WIN), :]
    s = _qk(q_ref[...], kw) * (DH_B ** -0.5) + bias_ref[0]
    w = lax.broadcasted_iota(jnp.int32, s.shape, 1)
    s = jnp.where(w + qi * bq >= BAND_ROWS, s, NEG)
    p = jnp.exp(s - jnp.max(s, axis=-1, keepdims=True))
    l = jnp.sum(p, axis=-1, keepdims=True)
    o = jnp.dot(p.astype(BF16), vw, preferred_element_type=F32) / l
    o_ref[...] = o.astype(o_ref.dtype)


def _attn_b_prompt(qb, k16, v16, bias_win, batch, seq):
    bq = ATT_BQ
    nq = seq // bq
    kv = pl.BlockSpec((None, seq, HEAD_LANES), lambda b, h, i: (h, b, 0))
    return pl.pallas_call(
        _attn_b_prompt_kernel,
        grid=(batch, H_B, nq),
        in_specs=[
            pl.BlockSpec((bq, HEAD_LANES), lambda b, h, i: (b * nq + i, h)),
            kv, kv,
            pl.BlockSpec((1, bq, BAND_WIN), lambda b, h, i: (h, 0, 0)),
        ],
        out_specs=pl.BlockSpec((bq, HEAD_LANES), lambda b, h, i: (b * nq + i, h)),
        out_shape=jax.ShapeDtypeStruct((batch * seq, W_B), BF16),
        scratch_shapes=[pltpu.VMEM((BAND_ROWS + seq, HEAD_LANES), BF16),
                        pltpu.VMEM((BAND_ROWS + seq, HEAD_LANES), BF16)],
        compiler_params=_params("parallel", "parallel", "arbitrary"),
        name="attn_b_prompt",
    )(qb, k16, v16, bias_win)


def _attn_b_sample_kernel(q_ref, kc_ref, vc_ref, kn_ref, vn_ref, bias_ref, o_ref):
    past = kc_ref.shape[0]
    scale = DH_B ** -0.5
    for h in range(H_B):
        cols = slice(h * HEAD_LANES, (h + 1) * HEAD_LANES)
        q = q_ref[:, cols]
        bias = bias_ref[h]
        sc = _qk(q, kc_ref[:, h, :].astype(BF16)) * scale + bias[:, :past]
        sn = _qk(q, kn_ref[h]) * scale + bias[:, past:]
        acc, l = _softmax_two_parts(sc, sn, vc_ref[:, h, :].astype(BF16), vn_ref[h])
        o_ref[:, cols] = (acc / l).astype(o_ref.dtype)


def _attn_b_sample(qb, cache_k, cache_v, k16, v16, bias, layer):
    _, nb, past, _, _ = cache_k.shape
    n = qb.shape[0] // nb
    rows = pl.BlockSpec((n, W_B), lambda b: (b, 0))
    new = pl.BlockSpec((H_B, n, HEAD_LANES), lambda b: (0, b, 0))
    old = pl.BlockSpec((None, None, past, H_B, HEAD_LANES), lambda b: (layer, b, 0, 0, 0))
    return pl.pallas_call(
        _attn_b_sample_kernel,
        grid=(nb,),
        in_specs=[rows, old, old, new, new,
                  pl.BlockSpec((H_B, n, past + n), lambda b: (0, 0, 0))],
        out_specs=rows,
        out_shape=jax.ShapeDtypeStruct(qb.shape, BF16),
        compiler_params=_params("parallel"),
        name="attn_b_sample",
    )(qb, cache_k, cache_v, k16, v16, bias)


def _merge_kernel(x_ref, oa_ref, ob_ref, sga_ref, sgb_ref, wa_ref, wb_ref, wo_ref, o_ref):
    a = jnp.dot(oa_ref[...], wa_ref[...], preferred_element_type=F32)
    b = jnp.dot(ob_ref[...], wb_ref[...], preferred_element_type=F32)
    merged = sga_ref[...].astype(F32) * a + sgb_ref[...].astype(F32) * b
    o_ref[...] = x_ref[...] + jnp.dot(merged.astype(BF16), wo_ref[...], preferred_element_type=F32)


def _merge(x, oa, ob, sga, sgb, wa, wb, wo):
    m = x.shape[0]
    bm = MERGE_BM
    wide = pl.BlockSpec((bm, D_MODEL), lambda i: (i, 0))
    half = pl.BlockSpec((bm, W_A), lambda i: (i, 0))

    def whole(shape):
        return pl.BlockSpec(shape, lambda i: (0, 0))

    return pl.pallas_call(
        _merge_kernel,
        grid=(m // bm,),
        in_specs=[wide, half, half, wide, wide,
                  whole((W_A, D_MODEL)), whole((W_B, D_MODEL)), whole((D_MODEL, D_MODEL))],
        out_specs=wide,
        out_shape=jax.ShapeDtypeStruct((m, D_MODEL), F32),
        compiler_params=_params("parallel"),
        name="merge",
    )(x, oa, ob, sga, sgb, wa, wb, wo)


def _rope_tables(pos):
    half = DK_A // 2
    inv = ROPE_THETA ** (-jnp.arange(half, dtype=F32) / half)
    ang = pos.astype(F32)[:, None] * inv[None, :]
    cos, sin = jnp.cos(ang), jnp.sin(ang)
    reps = HEAD_LANES // DK_A
    cos_t = jnp.tile(jnp.concatenate([cos, cos], axis=1), (1, reps))
    sin_t = jnp.tile(jnp.concatenate([-sin, sin], axis=1), (1, reps))
    return cos_t, sin_t


def _band_bias(table, nq, nk, q_offset):
    rel = jnp.clip((jnp.arange(nq) + q_offset)[:, None] - jnp.arange(nk)[None, :], -MAX_REL, MAX_REL) + MAX_REL
    return table[:, rel].astype(F32)


def kernel(x_prompt, x_sample, cache_a_k, cache_a_v, cache_b_k, cache_b_v, ffn1_norm, ffn1_w_gate, ffn1_w_up, ffn1_w_down, mix_norm, w_in, lambda_q1, lambda_k1, lambda_q2, lambda_k2, subln_a, rel_bias_b, w_branch_a, w_branch_b, w_out, ffn2_norm, ffn2_w_gate, ffn2_w_up, ffn2_w_down, final_norm):
    batch, seq, _ = x_prompt.shape
    dec_batch, dec_seq, _ = x_sample.shape
    depth = ffn1_norm.shape[0]
    past_a = cache_a_k.shape[2]
    past_b = cache_b_k.shape[2]
    assert seq % max(ATT_BQ, PROJ_BM) == 0 and seq >= BAND_ROWS
    assert dec_seq == CHUNK and past_b == BAND_ROWS and PROJ_BM % dec_seq == 0

    xp = x_prompt.reshape(batch * seq, D_MODEL)
    xs = x_sample.reshape(dec_batch * dec_seq, D_MODEL)
    rope_p = _rope_tables(jnp.arange(seq))
    rope_s = _rope_tables(past_a + (jnp.arange(PROJ_BM) % dec_seq))

    qc = jnp.arange(ATT_BQ) // CHUNK
    wc = jnp.arange(BAND_WIN) // CHUNK
    band_ok = (wc[None, :] >= qc[:, None]) & (wc[None, :] <= qc[:, None] + N_PREV)

    outs = {k: [] for k in ("akp", "avp", "bkp", "bvp", "aks", "avs", "bks", "bvs")}
    for layer in range(depth):
        lam_init = 0.8 - 0.6 * math.exp(-0.3 * layer)
        lam = (jnp.exp(jnp.sum(lambda_q1[layer].astype(F32) * lambda_k1[layer].astype(F32)))
               - jnp.exp(jnp.sum(lambda_q2[layer].astype(F32) * lambda_k2[layer].astype(F32)))
               + lam_init).reshape(1)
        lam_scale = 1.0 - lam_init
        g1 = ffn1_norm[layer].reshape(1, D_MODEL)
        gm = mix_norm[layer].reshape(1, D_MODEL)
        g2 = ffn2_norm[layer].reshape(1, D_MODEL)
        gf = final_norm.reshape(1, D_MODEL)
        subln = subln_a[layer].reshape(1, DV_A)
        w1g, w1u, w1d = (w[layer].astype(BF16) for w in (ffn1_w_gate, ffn1_w_up, ffn1_w_down))
        w2g, w2u, w2d = (w[layer].astype(BF16) for w in (ffn2_w_gate, ffn2_w_up, ffn2_w_down))
        win = w_in[layer].astype(BF16)
        wa, wb, wo = (w[layer].astype(BF16) for w in (w_branch_a, w_branch_b, w_out))
        table = rel_bias_b[layer]
        bias_s = _band_bias(table, dec_seq, past_b + dec_seq, past_b)
        bias_p = jnp.where(band_ok[None], _band_bias(table, ATT_BQ, BAND_WIN, BAND_ROWS), NEG)
        last = layer == depth - 1

        def trunk(x, rope_tabs, rope_period, attend):
            x1, hn = _ffn(x, g1, w1g, w1u, w1d, gm, "mix")
            c = 0
            qa = _proj(hn, win, c, W_QK_A, "rope_q", rope_tabs, rope_period); c += W_QK_A
            ka, ka16 = _proj(hn, win, c, W_QK_A, "rope_kv", rope_tabs, rope_period); c += W_QK_A
            va, va16 = _proj(hn, win, c, W_A, "kv"); c += W_A
            qb = _proj(hn, win, c, W_B, "plain"); c += W_B
            kb, kb16 = _proj(hn, win, c, W_B, "kv"); c += W_B
            vb, vb16 = _proj(hn, win, c, W_B, "kv"); c += W_B
            sga = _proj(hn, win, c, D_MODEL, "sigmoid"); c += D_MODEL
            sgb = _proj(hn, win, c, D_MODEL, "sigmoid")
            oa, ob = attend(qa, ka16, va16, qb, kb16, vb16)
            x2 = _merge(x1, oa, ob, sga, sgb, wa, wb, wo)
            if last:
                xo = _ffn(x2, g2, w2g, w2u, w2d, gf, "final")
            else:
                xo = _ffn(x2, g2, w2g, w2u, w2d, gf, "mix")[0]
            return xo, ka, va, kb, vb

        def attend_prompt(qa, ka16, va16, qb, kb16, vb16):
            oa = _attn_a_prompt(lam, qa, ka16, va16, subln, batch, seq, lam_scale)
            ob = _attn_b_prompt(qb, kb16, vb16, bias_p, batch, seq)
            return oa, ob

        def attend_sample(qa, ka16, va16, qb, kb16, vb16):
            oa = _attn_a_sample(lam, qa, cache_a_k, cache_a_v, ka16, va16, subln, layer, lam_scale)
            ob = _attn_b_sample(qb, cache_b_k, cache_b_v, kb16, vb16, bias_s, layer)
            return oa, ob

        xp, ka, va, kb, vb = trunk(xp, rope_p, seq, attend_prompt)
        outs["akp"].append(ka.reshape(batch, seq, H_A, 2 * DK_A))
        outs["avp"].append(va.reshape(batch, seq, H_A, DV_A))
        rows = min(BAND_ROWS, seq)
        outs["bkp"].append(kb.reshape(batch, seq, H_B, DH_B)[:, seq - rows:])
        outs["bvp"].append(vb.reshape(batch, seq, H_B, DH_B)[:, seq - rows:])

        xs, ka, va, kb, vb = trunk(xs, rope_s, PROJ_BM, attend_sample)
        outs["aks"].append(ka.reshape(dec_batch, dec_seq, H_A, 2 * DK_A))
        outs["avs"].append(va.reshape(dec_batch, dec_seq, H_A, DV_A))
        kb_all = jnp.concatenate([cache_b_k[layer], kb.reshape(dec_batch, dec_seq, H_B, DH_B)], axis=1)
        vb_all = jnp.concatenate([cache_b_v[layer], vb.reshape(dec_batch, dec_seq, H_B, DH_B)], axis=1)
        outs["bks"].append(kb_all[:, dec_seq:])
        outs["bvs"].append(vb_all[:, dec_seq:])

    y_prompt = xp.reshape(batch, seq, D_MODEL)
    y_sample = xs.reshape(dec_batch, dec_seq, D_MODEL)
    return (y_prompt, y_sample,
            jnp.stack(outs["akp"]), jnp.stack(outs["avp"]), jnp.stack(outs["bkp"]), jnp.stack(outs["bvp"]),
            jnp.stack(outs["aks"]), jnp.stack(outs["avs"]), jnp.stack(outs["bks"]), jnp.stack(outs["bvs"]))
```

```python
import functools
import math

import jax
import jax.numpy as jnp
from jax import lax
from jax.experimental import pallas as pl
from jax.experimental.pallas import tpu as pltpu

F32 = jnp.float32
BF16 = jnp.bfloat16

D_MODEL = 2048
CHUNK = 64
H_A = 8
DK_A = 64
DV_A = 2 * DK_A
W_QK_A = H_A * 2 * DK_A
W_A = H_A * DV_A
H_B = 8
DH_B = 128
W_B = H_B * DH_B
N_PREV = 8
BAND_ROWS = N_PREV * CHUNK
MAX_REL = 128
FFN_DIM = 4 * D_MODEL
ROPE_THETA = 10000.0
EPS = 1e-6
NEG = -1e30

HEAD_LANES = 128
VMEM_LIMIT = 52 * 1024 * 1024

FFN_BM = 512
FFN_BF = 512
PROJ_BM = 512
PROJ_BN = 1024
MERGE_BM = 256
ATT_BQ = 256
BAND_WIN = BAND_ROWS + ATT_BQ
BIAS_RULER = BAND_WIN + ATT_BQ


def _params(*sem):
    return pltpu.CompilerParams(dimension_semantics=sem, vmem_limit_bytes=VMEM_LIMIT)


def _rms(x, g):
    return x * lax.rsqrt(jnp.mean(x * x, axis=-1, keepdims=True) + EPS) * g


def _ffn_kernel(x_ref, g_ref, wg_ref, wu_ref, wd_ref, g2_ref, *rest, mode):
    if mode == "mix":
        o_ref, hn_ref, h_ref = rest
    else:
        o_ref, h_ref = rest
    f = pl.program_id(1)

    @pl.when(f == 0)
    def _():
        x = x_ref[...]
        h_ref[...] = _rms(x, g_ref[...]).astype(BF16)
        o_ref[...] = x

    h = h_ref[...]
    a = jnp.dot(h, wg_ref[...], preferred_element_type=F32)
    b = jnp.dot(h, wu_ref[...], preferred_element_type=F32)
    t = (0.5 * (a * jax.nn.sigmoid(a)) * b).astype(BF16)
    o_ref[...] += jnp.dot(t, wd_ref[...], preferred_element_type=F32)

    @pl.when(f == pl.num_programs(1) - 1)
    def _():
        y = _rms(o_ref[...], g2_ref[...])
        if mode == "mix":
            hn_ref[...] = y.astype(BF16)
        else:
            o_ref[...] = y


def _ffn(x, g, wg, wu, wd, g2, mode):
    m = x.shape[0]
    bm, bf = FFN_BM, FFN_BF
    grid = (m // bm, FFN_DIM // bf)
    row = pl.BlockSpec((bm, D_MODEL), lambda i, f: (i, 0))
    vec = pl.BlockSpec((1, D_MODEL), lambda i, f: (0, 0))
    in_specs = [
        row, vec,
        pl.BlockSpec((D_MODEL, bf), lambda i, f: (0, f)),
        pl.BlockSpec((D_MODEL, bf), lambda i, f: (0, f)),
        pl.BlockSpec((bf, D_MODEL), lambda i, f: (f, 0)),
        vec,
    ]
    if mode == "mix":
        out_shape = (jax.ShapeDtypeStruct((m, D_MODEL), F32), jax.ShapeDtypeStruct((m, D_MODEL), BF16))
        out_specs = (row, row)
    else:
        out_shape = jax.ShapeDtypeStruct((m, D_MODEL), F32)
        out_specs = row
    return pl.pallas_call(
        functools.partial(_ffn_kernel, mode=mode),
        grid=grid, in_specs=in_specs, out_specs=out_specs, out_shape=out_shape,
        scratch_shapes=[pltpu.VMEM((bm, D_MODEL), BF16)],
        compiler_params=_params("parallel", "arbitrary"),
        name="ffn_" + mode,
    )(x, g, wg, wu, wd, g2)


def _rope_slabs(z, cos, sin_signed):
    lane = lax.broadcasted_iota(jnp.int32, cos.shape, 1)
    first_half = (lane % DK_A) < (DK_A // 2)
    outs = []
    for s in range(z.shape[1] // HEAD_LANES):
        slab = z[:, s * HEAD_LANES:(s + 1) * HEAD_LANES]
        partner = jnp.where(first_half,
                            pltpu.roll(slab, HEAD_LANES - DK_A // 2, 1),
                            pltpu.roll(slab, DK_A // 2, 1))
        outs.append(slab * cos + partner * sin_signed)
    return jnp.concatenate(outs, axis=1)


def _proj_kernel(h_ref, w_ref, *rest, epilogue):
    rope = epilogue in ("rope_q", "rope_kv")
    heads = epilogue in ("rope_kv", "kv")
    if rope:
        cos_ref, sin_ref, *rest = rest
    z = jnp.dot(h_ref[...], w_ref[...], preferred_element_type=F32)
    if rope:
        z = _rope_slabs(z, cos_ref[...], sin_ref[...])
    if epilogue == "rope_q":
        z = z * (DK_A ** -0.5)
    elif epilogue == "sigmoid":
        z = jax.nn.sigmoid(z)
    if heads:
        o_ref, o16_ref = rest
        nh = z.shape[1] // HEAD_LANES
        for h in range(nh):
            zh = z[:, h * HEAD_LANES:(h + 1) * HEAD_LANES]
            o_ref[pl.ds(h, z.shape[0], stride=nh), :] = zh
            o16_ref[h] = zh.astype(BF16)
    else:
        (o_ref,) = rest
        o_ref[...] = z.astype(o_ref.dtype)


def _proj(hn, w_in, col0, ncols, epilogue, rope_tabs=None, rope_period=None):
    m = hn.shape[0]
    bm, bn = PROJ_BM, PROJ_BN
    joff = col0 // bn
    in_specs = [
        pl.BlockSpec((bm, D_MODEL), lambda i, j: (i, 0)),
        pl.BlockSpec((D_MODEL, bn), lambda i, j: (0, j + joff)),
    ]
    args = [hn, w_in]
    if rope_tabs is not None:
        nper = rope_period // bm
        tab = pl.BlockSpec((bm, HEAD_LANES), lambda i, j: (i % nper, 0))
        in_specs += [tab, tab]
        args += list(rope_tabs)
    if epilogue in ("rope_kv", "kv"):
        assert ncols == bn
        nh = bn // HEAD_LANES
        out_specs = (pl.BlockSpec((bm * nh, HEAD_LANES), lambda i, j: (i, 0)),
                     pl.BlockSpec((nh, bm, HEAD_LANES), lambda i, j: (0, i, 0)))
        out_shape = (jax.ShapeDtypeStruct((m * nh, HEAD_LANES), F32),
                     jax.ShapeDtypeStruct((nh, m, HEAD_LANES), BF16))
    else:
        out_specs = pl.BlockSpec((bm, bn), lambda i, j: (i, j))
        out_shape = jax.ShapeDtypeStruct((m, ncols), BF16)
    return pl.pallas_call(
        functools.partial(_proj_kernel, epilogue=epilogue),
        grid=(m // bm, ncols // bn), in_specs=in_specs,
        out_specs=out_specs, out_shape=out_shape,
        compiler_params=_params("parallel", "arbitrary"),
        name="proj_" + epilogue,
    )(*args)


def _block_diag_q(q):
    lane = lax.broadcasted_iota(jnp.int32, q.shape, 1)
    zero = jnp.zeros_like(q)
    return jnp.concatenate([jnp.where(lane < DK_A, q, zero), jnp.where(lane >= DK_A, q, zero)], axis=0)


def _qk(q, k):
    return lax.dot_general(q, k, (((1,), (1,)), ((), ())), preferred_element_type=F32)


def _diff_finish(acc, l, lam, subln, n):
    o = acc / l
    o = o[:n] - lam * o[n:]
    return _rms(o, subln)


def _attn_a_prompt_kernel(lam_ref, q_ref, kb_ref, vb_ref, subln_ref, o_ref, *, lam_scale):
    qi = pl.program_id(2)
    bq = q_ref.shape[0]
    qbd = _block_diag_q(q_ref[...])

    def step(j, carry, masked):
        m, l, acc = carry
        start = pl.multiple_of(j * bq, bq)
        ks = kb_ref[pl.ds(start, bq), :]
        vs = vb_ref[pl.ds(start, bq), :]
        s = _qk(qbd, ks)
        if masked:
            r = lax.broadcasted_iota(jnp.int32, s.shape, 0) % bq
            c = lax.broadcasted_iota(jnp.int32, s.shape, 1)
            s = jnp.where((c // CHUNK) <= (r // CHUNK), s, NEG)
        m_new = jnp.maximum(m, jnp.max(s, axis=-1, keepdims=True))
        alpha = jnp.exp(m - m_new)
        p = jnp.exp(s - m_new)
        l = alpha * l + jnp.sum(p, axis=-1, keepdims=True)
        acc = alpha * acc + jnp.dot(p.astype(BF16), vs, preferred_element_type=F32)
        return m_new, l, acc

    init = (jnp.full((2 * bq, 1), -jnp.inf, F32), jnp.zeros((2 * bq, 1), F32),
            jnp.zeros((2 * bq, HEAD_LANES), F32))
    carry = lax.fori_loop(0, qi, functools.partial(step, masked=False), init)
    _, l, acc = step(qi, carry, masked=True)
    o = _diff_finish(acc, l, lam_ref[0], subln_ref[...], bq) * lam_scale
    o_ref[...] = o.astype(o_ref.dtype)


def _attn_a_prompt(lam, qa, k16, v16, subln, batch, seq, lam_scale):
    bq = ATT_BQ
    nq = seq // bq
    kv = pl.BlockSpec((None, seq, HEAD_LANES), lambda b, h, i: (h, b, 0))
    return pl.pallas_call(
        functools.partial(_attn_a_prompt_kernel, lam_scale=lam_scale),
        grid=(batch, H_A, nq),
        in_specs=[
            pl.BlockSpec(memory_space=pltpu.SMEM),
            pl.BlockSpec((bq, HEAD_LANES), lambda b, h, i: (b * nq + i, h)),
            kv, kv,
            pl.BlockSpec((1, DV_A), lambda b, h, i: (0, 0)),
        ],
        out_specs=pl.BlockSpec((bq, HEAD_LANES), lambda b, h, i: (b * nq + i, h)),
        out_shape=jax.ShapeDtypeStruct((batch * seq, W_A), BF16),
        compiler_params=_params("parallel", "parallel", "arbitrary"),
        name="attn_a_prompt",
    )(lam, qa, k16, v16, subln)


def _softmax_two_parts(sc, sn, vc, vn):
    m = jnp.maximum(jnp.max(sc, axis=-1, keepdims=True), jnp.max(sn, axis=-1, keepdims=True))
    pc = jnp.exp(sc - m)
    pn = jnp.exp(sn - m)
    l = jnp.sum(pc, axis=-1, keepdims=True) + jnp.sum(pn, axis=-1, keepdims=True)
    acc = (jnp.dot(pc.astype(BF16), vc, preferred_element_type=F32)
           + jnp.dot(pn.astype(BF16), vn, preferred_element_type=F32))
    return acc, l


def _attn_a_sample_kernel(lam_ref, q_ref, kc_ref, vc_ref, kn_ref, vn_ref, subln_ref, o_ref, *, lam_scale):
    n = q_ref.shape[0]
    past = kc_ref.shape[0] // H_A
    for h in range(H_A):
        cols = slice(h * HEAD_LANES, (h + 1) * HEAD_LANES)
        head_rows = pl.ds(h, past, stride=H_A)
        qbd = _block_diag_q(q_ref[:, cols])
        sc = _qk(qbd, kc_ref[head_rows, :].astype(BF16))
        sn = _qk(qbd, kn_ref[h])
        acc, l = _softmax_two_parts(sc, sn, vc_ref[head_rows, :].astype(BF16), vn_ref[h])
        o = _diff_finish(acc, l, lam_ref[0], subln_ref[...], n) * lam_scale
        o_ref[:, cols] = o.astype(o_ref.dtype)


def _attn_a_sample(lam, qa, cache_k, cache_v, k16, v16, subln, layer, lam_scale):
    _, nb, cache_rows, _ = cache_k.shape
    n = qa.shape[0] // nb
    rows = pl.BlockSpec((n, W_A), lambda b: (b, 0))
    new = pl.BlockSpec((H_A, n, HEAD_LANES), lambda b: (0, b, 0))
    old = pl.BlockSpec((None, None, cache_rows, HEAD_LANES), lambda b: (layer, b, 0, 0))
    return pl.pallas_call(
        functools.partial(_attn_a_sample_kernel, lam_scale=lam_scale),
        grid=(nb,),
        in_specs=[pl.BlockSpec(memory_space=pltpu.SMEM), rows, old, old, new, new,
                  pl.BlockSpec((1, DV_A), lambda b: (0, 0))],
        out_specs=rows,
        out_shape=jax.ShapeDtypeStruct(qa.shape, BF16),
        compiler_params=_params("parallel"),
        name="attn_a_sample",
    )(lam, qa, cache_k, cache_v, k16, v16, subln)


def _toeplitz_bias(ruler_row, nq, nk):
    rows = jnp.broadcast_to(ruler_row, (nq, ruler_row.shape[1]))
    return pltpu.roll(rows, 0, 1, stride=1, stride_axis=0)[:, :nk]


def _attn_b_prompt_kernel(q_ref, k_ref, v_ref, ruler_ref, o_ref, kb_ref, vb_ref, bias_ref):
    qi = pl.program_id(2)
    bq = q_ref.shape[0]
    seq = k_ref.shape[0]

    @pl.when(qi == 0)
    def _():
        zeros = jnp.zeros((BAND_ROWS, HEAD_LANES), BF16)
        kb_ref[0:BAND_ROWS, :] = zeros
        vb_ref[0:BAND_ROWS, :] = zeros
        kb_ref[BAND_ROWS:BAND_ROWS + seq, :] = k_ref[...]
        vb_ref[BAND_ROWS:BAND_ROWS + seq, :] = v_ref[...]
        d = (lax.broadcasted_iota(jnp.int32, (bq, BAND_WIN), 1) // CHUNK
             - lax.broadcasted_iota(jnp.int32, (bq, BAND_WIN), 0) // CHUNK)
        bias = _toeplitz_bias(ruler_ref[...], bq, BAND_WIN)
        bias_ref[...] = jnp.where(d >= 0, jnp.where(d <= N_PREV, bias, NEG), NEG)

    start = pl.multiple_of(qi * bq, bq)
    kw = kb_ref[pl.ds(start, BAND_WIN), :]
    vw = vb_ref[pl.ds(start, BAND_WIN), :]
    s = _qk(q_ref[...], kw) * (DH_B ** -0.5) + bias_ref[...]
    w = lax.broadcasted_iota(jnp.int32, s.shape, 1)
    s = jnp.where(w + qi * bq >= BAND_ROWS, s, NEG)
    p = jnp.exp(s - jnp.max(s, axis=-1, keepdims=True))
    l = jnp.sum(p, axis=-1, keepdims=True)
    o = jnp.dot(p.astype(BF16), vw, preferred_element_type=F32) / l
    o_ref[...] = o.astype(o_ref.dtype)


def _attn_b_prompt(qb, k16, v16, ruler, batch, seq):
    bq = ATT_BQ
    nq = seq // bq
    kv = pl.BlockSpec((None, seq, HEAD_LANES), lambda b, h, i: (h, b, 0))
    return pl.pallas_call(
        _attn_b_prompt_kernel,
        grid=(batch, H_B, nq),
        in_specs=[
            pl.BlockSpec((bq, HEAD_LANES), lambda b, h, i: (b * nq + i, h)),
            kv, kv,
            pl.BlockSpec((None, 1, BIAS_RULER), lambda b, h, i: (h, 0, 0)),
        ],
        out_specs=pl.BlockSpec((bq, HEAD_LANES), lambda b, h, i: (b * nq + i, h)),
        out_shape=jax.ShapeDtypeStruct((batch * seq, W_B), BF16),
        scratch_shapes=[pltpu.VMEM((BAND_ROWS + seq, HEAD_LANES), BF16),
                        pltpu.VMEM((BAND_ROWS + seq, HEAD_LANES), BF16),
                        pltpu.VMEM((bq, BAND_WIN), F32)],
        compiler_params=_params("parallel", "parallel", "arbitrary"),
        name="attn_b_prompt",
    )(qb, k16, v16, ruler)


def _attn_b_sample_kernel(q_ref, kc_ref, vc_ref, kn_ref, vn_ref, ruler_ref, o_ref, bias_ref):
    n = q_ref.shape[0]
    past = kc_ref.shape[0] // H_B
    scale = DH_B ** -0.5

    @pl.when(pl.program_id(0) == 0)
    def _():
        for h in range(H_B):
            bias_ref[h] = _toeplitz_bias(ruler_ref[h], n, past + n)

    for h in range(H_B):
        cols = slice(h * HEAD_LANES, (h + 1) * HEAD_LANES)
        head_rows = pl.ds(h, past, stride=H_B)
        q = q_ref[:, cols]
        sc = _qk(q, kc_ref[head_rows, :].astype(BF16)) * scale + bias_ref[h, :, :past]
        sn = _qk(q, kn_ref[h]) * scale + bias_ref[h, :, past:]
        acc, l = _softmax_two_parts(sc, sn, vc_ref[head_rows, :].astype(BF16), vn_ref[h])
        o_ref[:, cols] = (acc / l).astype(o_ref.dtype)


def _attn_b_sample(qb, cache_k, cache_v, k16, v16, ruler, layer):
    _, nb, cache_rows, _ = cache_k.shape
    past = cache_rows // H_B
    n = qb.shape[0] // nb
    rows = pl.BlockSpec((n, W_B), lambda b: (b, 0))
    new = pl.BlockSpec((H_B, n, HEAD_LANES), lambda b: (0, b, 0))
    old = pl.BlockSpec((None, None, cache_rows, HEAD_LANES), lambda b: (layer, b, 0, 0))
    return pl.pallas_call(
        _attn_b_sample_kernel,
        grid=(nb,),
        in_specs=[rows, old, old, new, new,
                  pl.BlockSpec((H_B, 1, BIAS_RULER), lambda b: (0, 0, 0))],
        out_specs=rows,
        out_shape=jax.ShapeDtypeStruct(qb.shape, BF16),
        scratch_shapes=[pltpu.VMEM((H_B, n, past + n), F32)],
        compiler_params=_params("arbitrary"),
        name="attn_b_sample",
    )(qb, cache_k, cache_v, k16, v16, ruler)


def _merge_kernel(x_ref, oa_ref, ob_ref, sga_ref, sgb_ref, wa_ref, wb_ref, wo_ref, o_ref):
    a = jnp.dot(oa_ref[...], wa_ref[...], preferred_element_type=F32)
    b = jnp.dot(ob_ref[...], wb_ref[...], preferred_element_type=F32)
    merged = sga_ref[...].astype(F32) * a + sgb_ref[...].astype(F32) * b
    o_ref[...] = x_ref[...] + jnp.dot(merged.astype(BF16), wo_ref[...], preferred_element_type=F32)


def _merge(x, oa, ob, sga, sgb, wa, wb, wo):
    m = x.shape[0]
    bm = MERGE_BM
    wide = pl.BlockSpec((bm, D_MODEL), lambda i: (i, 0))
    half = pl.BlockSpec((bm, W_A), lambda i: (i, 0))

    def whole(shape):
        return pl.BlockSpec(shape, lambda i: (0, 0))

    return pl.pallas_call(
        _merge_kernel,
        grid=(m // bm,),
        in_specs=[wide, half, half, wide, wide,
                  whole((W_A, D_MODEL)), whole((W_B, D_MODEL)), whole((D_MODEL, D_MODEL))],
        out_specs=wide,
        out_shape=jax.ShapeDtypeStruct((m, D_MODEL), F32),
        compiler_params=_params("parallel"),
        name="merge",
    )(x, oa, ob, sga, sgb, wa, wb, wo)


def _rope_tables(pos):
    half = DK_A // 2
    inv = ROPE_THETA ** (-jnp.arange(half, dtype=F32) / half)
    ang = pos.astype(F32)[:, None] * inv[None, :]
    cos, sin = jnp.cos(ang), jnp.sin(ang)
    reps = HEAD_LANES // DK_A
    cos_t = jnp.tile(jnp.concatenate([cos, cos], axis=1), (1, reps))
    sin_t = jnp.tile(jnp.concatenate([-sin, sin], axis=1), (1, reps))
    return cos_t, sin_t


def _bias_ruler(table):
    nh = table.shape[0]
    far_past = jnp.broadcast_to(table[:, 2 * MAX_REL:], (nh, BAND_ROWS - MAX_REL))
    far_future = jnp.broadcast_to(table[:, :1], (nh, BIAS_RULER - ATT_BQ - BAND_ROWS - MAX_REL - 1))
    wrapped = jnp.broadcast_to(table[:, 2 * MAX_REL:], (nh, ATT_BQ))
    ruler = jnp.concatenate([far_past, table[:, ::-1], far_future, wrapped], axis=1)
    return ruler.astype(F32).reshape(nh, 1, BIAS_RULER)


def kernel(x_prompt, x_sample, cache_a_k, cache_a_v, cache_b_k, cache_b_v, ffn1_norm, ffn1_w_gate, ffn1_w_up, ffn1_w_down, mix_norm, w_in, lambda_q1, lambda_k1, lambda_q2, lambda_k2, subln_a, rel_bias_b, w_branch_a, w_branch_b, w_out, ffn2_norm, ffn2_w_gate, ffn2_w_up, ffn2_w_down, final_norm):
    batch, seq, _ = x_prompt.shape
    dec_batch, dec_seq, _ = x_sample.shape
    depth = ffn1_norm.shape[0]
    past_a = cache_a_k.shape[2]
    past_b = cache_b_k.shape[2]
    assert seq % max(ATT_BQ, PROJ_BM) == 0 and seq >= BAND_ROWS
    assert dec_seq == CHUNK and past_b == BAND_ROWS and PROJ_BM % dec_seq == 0

    xp = x_prompt.reshape(batch * seq, D_MODEL)
    xs = x_sample.reshape(dec_batch * dec_seq, D_MODEL)
    rope_p = _rope_tables(jnp.arange(seq))
    rope_s = _rope_tables(past_a + (jnp.arange(PROJ_BM) % dec_seq))
    cache_a_k2 = cache_a_k.reshape(depth, dec_batch, past_a * H_A, HEAD_LANES)
    cache_a_v2 = cache_a_v.reshape(depth, dec_batch, past_a * H_A, HEAD_LANES)
    cache_b_k2 = cache_b_k.reshape(depth, dec_batch, past_b * H_B, HEAD_LANES)
    cache_b_v2 = cache_b_v.reshape(depth, dec_batch, past_b * H_B, HEAD_LANES)

    outs = {k: [] for k in ("akp", "avp", "bkp", "bvp", "aks", "avs", "bks", "bvs")}
    for layer in range(depth):
        lam_init = 0.8 - 0.6 * math.exp(-0.3 * layer)
        lam = (jnp.exp(jnp.sum(lambda_q1[layer].astype(F32) * lambda_k1[layer].astype(F32)))
               - jnp.exp(jnp.sum(lambda_q2[layer].astype(F32) * lambda_k2[layer].astype(F32)))
               + lam_init).reshape(1)
        lam_scale = 1.0 - lam_init
        g1 = ffn1_norm[layer].reshape(1, D_MODEL)
        gm = mix_norm[layer].reshape(1, D_MODEL)
        g2 = ffn2_norm[layer].reshape(1, D_MODEL)
        gf = final_norm.reshape(1, D_MODEL)
        subln = subln_a[layer].reshape(1, DV_A)
        w1g, w1u, w1d = (w[layer].astype(BF16) for w in (ffn1_w_gate, ffn1_w_up, ffn1_w_down))
        w2g, w2u, w2d = (w[layer].astype(BF16) for w in (ffn2_w_gate, ffn2_w_up, ffn2_w_down))
        win = w_in[layer].astype(BF16)
        wa, wb, wo = (w[layer].astype(BF16) for w in (w_branch_a, w_branch_b, w_out))
        table = rel_bias_b[layer]
        ruler = _bias_ruler(table)
        last = layer == depth - 1

        def trunk(x, rope_tabs, rope_period, attend):
            x1, hn = _ffn(x, g1, w1g, w1u, w1d, gm, "mix")
            c = 0
            qa = _proj(hn, win, c, W_QK_A, "rope_q", rope_tabs, rope_period); c += W_QK_A
            ka, ka16 = _proj(hn, win, c, W_QK_A, "rope_kv", rope_tabs, rope_period); c += W_QK_A
            va, va16 = _proj(hn, win, c, W_A, "kv"); c += W_A
            qb = _proj(hn, win, c, W_B, "plain"); c += W_B
            kb, kb16 = _proj(hn, win, c, W_B, "kv"); c += W_B
            vb, vb16 = _proj(hn, win, c, W_B, "kv"); c += W_B
            sga = _proj(hn, win, c, D_MODEL, "sigmoid"); c += D_MODEL
            sgb = _proj(hn, win, c, D_MODEL, "sigmoid")
            oa, ob = attend(qa, ka16, va16, qb, kb16, vb16)
            x2 = _merge(x1, oa, ob, sga, sgb, wa, wb, wo)
            if last:
                xo = _ffn(x2, g2, w2g, w2u, w2d, gf, "final")
            else:
                xo = _ffn(x2, g2, w2g, w2u, w2d, gf, "mix")[0]
            return xo, ka, va, kb, vb

        def attend_prompt(qa, ka16, va16, qb, kb16, vb16):
            oa = _attn_a_prompt(lam, qa, ka16, va16, subln, batch, seq, lam_scale)
            ob = _attn_b_prompt(qb, kb16, vb16, ruler, batch, seq)
            return oa, ob

        def attend_sample(qa, ka16, va16, qb, kb16, vb16):
            oa = _attn_a_sample(lam, qa, cache_a_k2, cache_a_v2, ka16, va16, subln, layer, lam_scale)
            ob = _attn_b_sample(qb, cache_b_k2, cache_b_v2, kb16, vb16, ruler, layer)
            return oa, ob

        xp, ka, va, kb, vb = trunk(xp, rope_p, seq, attend_prompt)
        outs["akp"].append(ka.reshape(batch, seq, H_A, 2 * DK_A))
        outs["avp"].append(va.reshape(batch, seq, H_A, DV_A))
        rows = min(BAND_ROWS, seq)
        outs["bkp"].append(kb.reshape(batch, seq, H_B, DH_B)[:, seq - rows:])
        outs["bvp"].append(vb.reshape(batch, seq, H_B, DH_B)[:, seq - rows:])

        xs, ka, va, kb, vb = trunk(xs, rope_s, PROJ_BM, attend_sample)
        outs["aks"].append(ka.reshape(dec_batch, dec_seq, H_A, 2 * DK_A))
        outs["avs"].append(va.reshape(dec_batch, dec_seq, H_A, DV_A))
        kb_all = jnp.concatenate([cache_b_k[layer], kb.reshape(dec_batch, dec_seq, H_B, DH_B)], axis=1)
        vb_all = jnp.concatenate([cache_b_v[layer], vb.reshape(dec_batch, dec_seq, H_B, DH_B)], axis=1)
        outs["bks"].append(kb_all[:, dec_seq:])
        outs["bvs"].append(vb_all[:, dec_seq:])

    y_prompt = xp.reshape(batch, seq, D_MODEL)
    y_sample = xs.reshape(dec_batch, dec_seq, D_MODEL)
    return (y_prompt, y_sample,
            jnp.stack(outs["akp"]), jnp.stack(outs["avp"]), jnp.stack(outs["bkp"]), jnp.stack(outs["bvp"]),
            jnp.stack(outs["aks"]), jnp.stack(outs["avs"]), jnp.stack(outs["bks"]), jnp.stack(outs["bvs"]))
```

```python
import functools
import math

import jax
import jax.numpy as jnp
from jax import lax
from jax.experimental import pallas as pl
from jax.experimental.pallas import tpu as pltpu

F32 = jnp.float32
BF16 = jnp.bfloat16

D_MODEL = 2048
CHUNK = 64
H_A = 8
DK_A = 64
DV_A = 2 * DK_A
W_QK_A = H_A * 2 * DK_A
W_A = H_A * DV_A
H_B = 8
DH_B = 128
W_B = H_B * DH_B
N_PREV = 8
BAND_ROWS = N_PREV * CHUNK
MAX_REL = 128
FFN_DIM = 4 * D_MODEL
ROPE_THETA = 10000.0
EPS = 1e-6
NEG = -1e30

HEAD_LANES = 128
VMEM_LIMIT = 52 * 1024 * 1024

FFN_BM = 512
FFN_BF = 512
PROJ_BM = 512
PROJ_BN = 1024
MERGE_BM = 256
ATT_BQ = 256
BAND_WIN = BAND_ROWS + ATT_BQ
BIAS_RULER = BAND_WIN + ATT_BQ


def _params(*sem):
    return pltpu.CompilerParams(dimension_semantics=sem, vmem_limit_bytes=VMEM_LIMIT)


def _rms(x, g):
    return x * lax.rsqrt(jnp.mean(x * x, axis=-1, keepdims=True) + EPS) * g


def _ffn_kernel(x_ref, g_ref, wg_ref, wu_ref, wd_ref, g2_ref, *rest, mode):
    if mode == "mix":
        o_ref, hn_ref, h_ref = rest
    else:
        o_ref, h_ref = rest
    f = pl.program_id(1)

    @pl.when(f == 0)
    def _():
        x = x_ref[...]
        h_ref[...] = _rms(x, g_ref[...]).astype(BF16)
        o_ref[...] = x

    h = h_ref[...]
    a = jnp.dot(h, wg_ref[...], preferred_element_type=F32)
    b = jnp.dot(h, wu_ref[...], preferred_element_type=F32)
    t = (0.5 * (a * jax.nn.sigmoid(a)) * b).astype(BF16)
    o_ref[...] += jnp.dot(t, wd_ref[...], preferred_element_type=F32)

    @pl.when(f == pl.num_programs(1) - 1)
    def _():
        y = _rms(o_ref[...], g2_ref[...])
        if mode == "mix":
            hn_ref[...] = y.astype(BF16)
        else:
            o_ref[...] = y


def _ffn(x, g, wg, wu, wd, g2, mode):
    m = x.shape[0]
    bm, bf = FFN_BM, FFN_BF
    grid = (m // bm, FFN_DIM // bf)
    row = pl.BlockSpec((bm, D_MODEL), lambda i, f: (i, 0))
    vec = pl.BlockSpec((1, D_MODEL), lambda i, f: (0, 0))
    in_specs = [
        row, vec,
        pl.BlockSpec((D_MODEL, bf), lambda i, f: (0, f)),
        pl.BlockSpec((D_MODEL, bf), lambda i, f: (0, f)),
        pl.BlockSpec((bf, D_MODEL), lambda i, f: (f, 0)),
        vec,
    ]
    if mode == "mix":
        out_shape = (jax.ShapeDtypeStruct((m, D_MODEL), F32), jax.ShapeDtypeStruct((m, D_MODEL), BF16))
        out_specs = (row, row)
    else:
        out_shape = jax.ShapeDtypeStruct((m, D_MODEL), F32)
        out_specs = row
    return pl.pallas_call(
        functools.partial(_ffn_kernel, mode=mode),
        grid=grid, in_specs=in_specs, out_specs=out_specs, out_shape=out_shape,
        scratch_shapes=[pltpu.VMEM((bm, D_MODEL), BF16)],
        compiler_params=_params("parallel", "arbitrary"),
        name="ffn_" + mode,
    )(x, g, wg, wu, wd, g2)


def _rope_slabs(z, cos, sin_signed):
    lane = lax.broadcasted_iota(jnp.int32, cos.shape, 1)
    first_half = (lane % DK_A) < (DK_A // 2)
    outs = []
    for s in range(z.shape[1] // HEAD_LANES):
        slab = z[:, s * HEAD_LANES:(s + 1) * HEAD_LANES]
        partner = jnp.where(first_half,
                            pltpu.roll(slab, HEAD_LANES - DK_A // 2, 1),
                            pltpu.roll(slab, DK_A // 2, 1))
        outs.append(slab * cos + partner * sin_signed)
    return jnp.concatenate(outs, axis=1)


def _proj_kernel(h_ref, w_ref, *rest, epilogue, transposed):
    rope = epilogue in ("rope_q", "rope_kv")
    heads = epilogue in ("rope_kv", "kv")
    if rope:
        cos_ref, sin_ref, *rest = rest
    z = jnp.dot(h_ref[...], w_ref[...], preferred_element_type=F32)
    if rope:
        z = _rope_slabs(z, cos_ref[...], sin_ref[...])
    if epilogue == "rope_q":
        z = z * (DK_A ** -0.5)
    elif epilogue == "sigmoid":
        z = jax.nn.sigmoid(z)
    nh = z.shape[1] // HEAD_LANES
    if heads:
        o_ref, o16_ref = rest
        for h in range(nh):
            zh = z[:, h * HEAD_LANES:(h + 1) * HEAD_LANES]
            o_ref[pl.ds(h, z.shape[0], stride=nh), :] = zh
            o16_ref[h] = (zh.T if transposed else zh).astype(BF16)
    elif transposed:
        (o_ref,) = rest
        for h in range(nh):
            o_ref[h] = z[:, h * HEAD_LANES:(h + 1) * HEAD_LANES].T.astype(BF16)
    else:
        (o_ref,) = rest
        o_ref[...] = z.astype(o_ref.dtype)


def _proj(hn, w_in, col0, ncols, epilogue, rope_tabs=None, rope_period=None, transposed=False):
    m = hn.shape[0]
    bm, bn = PROJ_BM, PROJ_BN
    joff = col0 // bn
    nh = bn // HEAD_LANES
    head_major = (pl.BlockSpec((nh, HEAD_LANES, bm), lambda i, j: (0, 0, i)) if transposed
                  else pl.BlockSpec((nh, bm, HEAD_LANES), lambda i, j: (0, i, 0)))
    head_major_shape = jax.ShapeDtypeStruct((nh, HEAD_LANES, m) if transposed else (nh, m, HEAD_LANES), BF16)
    in_specs = [
        pl.BlockSpec((bm, D_MODEL), lambda i, j: (i, 0)),
        pl.BlockSpec((D_MODEL, bn), lambda i, j: (0, j + joff)),
    ]
    args = [hn, w_in]
    if rope_tabs is not None:
        nper = rope_period // bm
        tab = pl.BlockSpec((bm, HEAD_LANES), lambda i, j: (i % nper, 0))
        in_specs += [tab, tab]
        args += list(rope_tabs)
    if epilogue in ("rope_kv", "kv"):
        assert ncols == bn
        out_specs = (pl.BlockSpec((bm * nh, HEAD_LANES), lambda i, j: (i, 0)), head_major)
        out_shape = (jax.ShapeDtypeStruct((m * nh, HEAD_LANES), F32), head_major_shape)
    elif transposed:
        assert ncols == bn
        out_specs, out_shape = head_major, head_major_shape
    else:
        out_specs = pl.BlockSpec((bm, bn), lambda i, j: (i, j))
        out_shape = jax.ShapeDtypeStruct((m, ncols), BF16)
    return pl.pallas_call(
        functools.partial(_proj_kernel, epilogue=epilogue, transposed=transposed),
        grid=(m // bm, ncols // bn), in_specs=in_specs,
        out_specs=out_specs, out_shape=out_shape,
        compiler_params=_params("parallel", "arbitrary"),
        name="proj_" + epilogue,
    )(*args)


def _block_diag_q(q):
    lane = lax.broadcasted_iota(jnp.int32, q.shape, 1)
    zero = jnp.zeros_like(q)
    return jnp.concatenate([jnp.where(lane < DK_A, q, zero), jnp.where(lane >= DK_A, q, zero)], axis=0)


def _qk(q, k):
    return lax.dot_general(q, k, (((1,), (1,)), ((), ())), preferred_element_type=F32)


def _diff_finish(acc, l, lam, subln, n):
    o = acc / l
    o = o[:n] - lam * o[n:]
    return _rms(o, subln)


def _attn_a_prompt_kernel(lam_ref, qt_ref, k_ref, vt_ref, subln_ref, o_ref, *, lam_scale):
    qi = pl.program_id(1)
    nh, _, bq = qt_ref.shape
    row = lax.broadcasted_iota(jnp.int32, (HEAD_LANES, bq), 0)
    qbd = []
    for h in range(nh):
        qt = qt_ref[h]
        zero = jnp.zeros_like(qt)
        qbd.append(jnp.concatenate([jnp.where(row < DK_A, qt, zero), jnp.where(row >= DK_A, qt, zero)], axis=1))

    def step(j, carries, masked):
        start = pl.multiple_of(j * bq, bq)
        scores = [jnp.dot(k_ref[h, pl.ds(start, bq), :], qbd[h], preferred_element_type=F32) for h in range(nh)]
        out = []
        for h in range(nh):
            m, l, acc = carries[h]
            s = scores[h]
            if masked:
                kr = lax.broadcasted_iota(jnp.int32, s.shape, 0)
                qc = lax.broadcasted_iota(jnp.int32, s.shape, 1) % bq
                s = jnp.where((kr // CHUNK) <= (qc // CHUNK), s, NEG)
            m_new = jnp.maximum(m, jnp.max(s, axis=0, keepdims=True))
            alpha = jnp.exp(m - m_new)
            p = jnp.exp(s - m_new)
            l = alpha * l + jnp.sum(p, axis=0, keepdims=True)
            acc = alpha * acc + jnp.dot(vt_ref[h, :, pl.ds(start, bq)], p.astype(BF16),
                                        preferred_element_type=F32)
            out.append((m_new, l, acc))
        return tuple(out)

    init = (jnp.full((1, 2 * bq), -jnp.inf, F32), jnp.zeros((1, 2 * bq), F32),
            jnp.zeros((HEAD_LANES, 2 * bq), F32))
    carries = lax.fori_loop(0, qi, functools.partial(step, masked=False), (init,) * nh)
    carries = step(qi, carries, masked=True)
    for h in range(nh):
        _, l, acc = carries[h]
        o = acc / l
        o = (o[:, :bq] - lam_ref[0] * o[:, bq:]).T
        o = _rms(o, subln_ref[...]) * lam_scale
        o_ref[:, h * HEAD_LANES:(h + 1) * HEAD_LANES] = o.astype(o_ref.dtype)


def _attn_a_prompt(lam, qt, k16, vt, subln, batch, seq, lam_scale):
    bq = ATT_BQ
    nq = seq // bq
    return pl.pallas_call(
        functools.partial(_attn_a_prompt_kernel, lam_scale=lam_scale),
        grid=(batch, nq),
        in_specs=[
            pl.BlockSpec(memory_space=pltpu.SMEM),
            pl.BlockSpec((H_A, HEAD_LANES, bq), lambda b, i: (0, 0, b * nq + i)),
            pl.BlockSpec((H_A, seq, HEAD_LANES), lambda b, i: (0, b, 0)),
            pl.BlockSpec((H_A, HEAD_LANES, seq), lambda b, i: (0, 0, b)),
            pl.BlockSpec((1, DV_A), lambda b, i: (0, 0)),
        ],
        out_specs=pl.BlockSpec((bq, W_A), lambda b, i: (b * nq + i, 0)),
        out_shape=jax.ShapeDtypeStruct((batch * seq, W_A), BF16),
        compiler_params=_params("parallel", "arbitrary"),
        name="attn_a_prompt",
    )(lam, qt, k16, vt, subln)


def _softmax_two_parts(sc, sn, vc, vn):
    m = jnp.maximum(jnp.max(sc, axis=-1, keepdims=True), jnp.max(sn, axis=-1, keepdims=True))
    pc = jnp.exp(sc - m)
    pn = jnp.exp(sn - m)
    l = jnp.sum(pc, axis=-1, keepdims=True) + jnp.sum(pn, axis=-1, keepdims=True)
    acc = (jnp.dot(pc.astype(BF16), vc, preferred_element_type=F32)
           + jnp.dot(pn.astype(BF16), vn, preferred_element_type=F32))
    return acc, l


def _attn_a_sample_kernel(lam_ref, q_ref, kc_ref, vc_ref, kn_ref, vn_ref, subln_ref, o_ref, *, lam_scale):
    n = q_ref.shape[0]
    past = kc_ref.shape[0] // H_A
    for h in range(H_A):
        cols = slice(h * HEAD_LANES, (h + 1) * HEAD_LANES)
        head_rows = pl.ds(h, past, stride=H_A)
        qbd = _block_diag_q(q_ref[:, cols])
        sc = _qk(qbd, kc_ref[head_rows, :].astype(BF16))
        sn = _qk(qbd, kn_ref[h])
        acc, l = _softmax_two_parts(sc, sn, vc_ref[head_rows, :].astype(BF16), vn_ref[h])
        o = _diff_finish(acc, l, lam_ref[0], subln_ref[...], n) * lam_scale
        o_ref[:, cols] = o.astype(o_ref.dtype)


def _attn_a_sample(lam, qa, cache_k, cache_v, k16, v16, subln, layer, lam_scale):
    _, nb, cache_rows, _ = cache_k.shape
    n = qa.shape[0] // nb
    rows = pl.BlockSpec((n, W_A), lambda b: (b, 0))
    new = pl.BlockSpec((H_A, n, HEAD_LANES), lambda b: (0, b, 0))
    old = pl.BlockSpec((None, None, cache_rows, HEAD_LANES), lambda b: (layer, b, 0, 0))
    return pl.pallas_call(
        functools.partial(_attn_a_sample_kernel, lam_scale=lam_scale),
        grid=(nb,),
        in_specs=[pl.BlockSpec(memory_space=pltpu.SMEM), rows, old, old, new, new,
                  pl.BlockSpec((1, DV_A), lambda b: (0, 0))],
        out_specs=rows,
        out_shape=jax.ShapeDtypeStruct(qa.shape, BF16),
        compiler_params=_params("parallel"),
        name="attn_a_sample",
    )(lam, qa, cache_k, cache_v, k16, v16, subln)


def _toeplitz_bias(ruler_row, nq, nk):
    rows = jnp.broadcast_to(ruler_row, (nq, ruler_row.shape[1]))
    return pltpu.roll(rows, 0, 1, stride=1, stride_axis=0)[:, :nk]


def _attn_b_prompt_kernel(q_ref, k_ref, v_ref, ruler_ref, o_ref, kb_ref, vb_ref, bias_ref):
    qi = pl.program_id(2)
    bq = q_ref.shape[0]
    seq = k_ref.shape[0]

    @pl.when(qi == 0)
    def _():
        zeros = jnp.zeros((BAND_ROWS, HEAD_LANES), BF16)
        kb_ref[0:BAND_ROWS, :] = zeros
        vb_ref[0:BAND_ROWS, :] = zeros
        kb_ref[BAND_ROWS:BAND_ROWS + seq, :] = k_ref[...]
        vb_ref[BAND_ROWS:BAND_ROWS + seq, :] = v_ref[...]
        d = (lax.broadcasted_iota(jnp.int32, (bq, BAND_WIN), 1) // CHUNK
             - lax.broadcasted_iota(jnp.int32, (bq, BAND_WIN), 0) // CHUNK)
        bias = _toeplitz_bias(ruler_ref[...], bq, BAND_WIN)
        bias_ref[...] = jnp.where(d >= 0, jnp.where(d <= N_PREV, bias, NEG), NEG)

    start = pl.multiple_of(qi * bq, bq)
    kw = kb_ref[pl.ds(start, BAND_WIN), :]
    vw = vb_ref[pl.ds(start, BAND_WIN), :]
    s = _qk(q_ref[...], kw) * (DH_B ** -0.5) + bias_ref[...]
    w = lax.broadcasted_iota(jnp.int32, s.shape, 1)
    s = jnp.where(w + qi * bq >= BAND_ROWS, s, NEG)
    p = jnp.exp(s - jnp.max(s, axis=-1, keepdims=True))
    l = jnp.sum(p, axis=-1, keepdims=True)
    o = jnp.dot(p.astype(BF16), vw, preferred_element_type=F32) / l
    o_ref[...] = o.astype(o_ref.dtype)


def _attn_b_prompt(qb, k16, v16, ruler, batch, seq):
    bq = ATT_BQ
    nq = seq // bq
    kv = pl.BlockSpec((None, seq, HEAD_LANES), lambda b, h, i: (h, b, 0))
    return pl.pallas_call(
        _attn_b_prompt_kernel,
        grid=(batch, H_B, nq),
        in_specs=[
            pl.BlockSpec((bq, HEAD_LANES), lambda b, h, i: (b * nq + i, h)),
            kv, kv,
            pl.BlockSpec((None, 1, BIAS_RULER), lambda b, h, i: (h, 0, 0)),
        ],
        out_specs=pl.BlockSpec((bq, HEAD_LANES), lambda b, h, i: (b * nq + i, h)),
        out_shape=jax.ShapeDtypeStruct((batch * seq, W_B), BF16),
        scratch_shapes=[pltpu.VMEM((BAND_ROWS + seq, HEAD_LANES), BF16),
                        pltpu.VMEM((BAND_ROWS + seq, HEAD_LANES), BF16),
                        pltpu.VMEM((bq, BAND_WIN), F32)],
        compiler_params=_params("parallel", "parallel", "arbitrary"),
        name="attn_b_prompt",
    )(qb, k16, v16, ruler)


def _attn_b_sample_kernel(q_ref, kc_ref, vc_ref, kn_ref, vn_ref, ruler_ref, o_ref, bias_ref):
    n = q_ref.shape[0]
    past = kc_ref.shape[0] // H_B
    scale = DH_B ** -0.5

    @pl.when(pl.program_id(0) == 0)
    def _():
        for h in range(H_B):
            bias_ref[h] = _toeplitz_bias(ruler_ref[h], n, past + n)

    for h in range(H_B):
        cols = slice(h * HEAD_LANES, (h + 1) * HEAD_LANES)
        head_rows = pl.ds(h, past, stride=H_B)
        q = q_ref[:, cols]
        sc = _qk(q, kc_ref[head_rows, :].astype(BF16)) * scale + bias_ref[h, :, :past]
        sn = _qk(q, kn_ref[h]) * scale + bias_ref[h, :, past:]
        acc, l = _softmax_two_parts(sc, sn, vc_ref[head_rows, :].astype(BF16), vn_ref[h])
        o_ref[:, cols] = (acc / l).astype(o_ref.dtype)


def _attn_b_sample(qb, cache_k, cache_v, k16, v16, ruler, layer):
    _, nb, cache_rows, _ = cache_k.shape
    past = cache_rows // H_B
    n = qb.shape[0] // nb
    rows = pl.BlockSpec((n, W_B), lambda b: (b, 0))
    new = pl.BlockSpec((H_B, n, HEAD_LANES), lambda b: (0, b, 0))
    old = pl.BlockSpec((None, None, cache_rows, HEAD_LANES), lambda b: (layer, b, 0, 0))
    return pl.pallas_call(
        _attn_b_sample_kernel,
        grid=(nb,),
        in_specs=[rows, old, old, new, new,
                  pl.BlockSpec((H_B, 1, BIAS_RULER), lambda b: (0, 0, 0))],
        out_specs=rows,
        out_shape=jax.ShapeDtypeStruct(qb.shape, BF16),
        scratch_shapes=[pltpu.VMEM((H_B, n, past + n), F32)],
        compiler_params=_params("arbitrary"),
        name="attn_b_sample",
    )(qb, cache_k, cache_v, k16, v16, ruler)


def _merge_kernel(x_ref, oa_ref, ob_ref, sga_ref, sgb_ref, wa_ref, wb_ref, wo_ref, o_ref):
    a = jnp.dot(oa_ref[...], wa_ref[...], preferred_element_type=F32)
    b = jnp.dot(ob_ref[...], wb_ref[...], preferred_element_type=F32)
    merged = sga_ref[...].astype(F32) * a + sgb_ref[...].astype(F32) * b
    o_ref[...] = x_ref[...] + jnp.dot(merged.astype(BF16), wo_ref[...], preferred_element_type=F32)


def _merge(x, oa, ob, sga, sgb, wa, wb, wo):
    m = x.shape[0]
    bm = MERGE_BM
    wide = pl.BlockSpec((bm, D_MODEL), lambda i: (i, 0))
    half = pl.BlockSpec((bm, W_A), lambda i: (i, 0))

    def whole(shape):
        return pl.BlockSpec(shape, lambda i: (0, 0))

    return pl.pallas_call(
        _merge_kernel,
        grid=(m // bm,),
        in_specs=[wide, half, half, wide, wide,
                  whole((W_A, D_MODEL)), whole((W_B, D_MODEL)), whole((D_MODEL, D_MODEL))],
        out_specs=wide,
        out_shape=jax.ShapeDtypeStruct((m, D_MODEL), F32),
        compiler_params=_params("parallel"),
        name="merge",
    )(x, oa, ob, sga, sgb, wa, wb, wo)


def _rope_tables(pos):
    half = DK_A // 2
    inv = ROPE_THETA ** (-jnp.arange(half, dtype=F32) / half)
    ang = pos.astype(F32)[:, None] * inv[None, :]
    cos, sin = jnp.cos(ang), jnp.sin(ang)
    reps = HEAD_LANES // DK_A
    cos_t = jnp.tile(jnp.concatenate([cos, cos], axis=1), (1, reps))
    sin_t = jnp.tile(jnp.concatenate([-sin, sin], axis=1), (1, reps))
    return cos_t, sin_t


def _bias_ruler(table):
    nh = table.shape[0]
    far_past = jnp.broadcast_to(table[:, 2 * MAX_REL:], (nh, BAND_ROWS - MAX_REL))
    far_future = jnp.broadcast_to(table[:, :1], (nh, BIAS_RULER - ATT_BQ - BAND_ROWS - MAX_REL - 1))
    wrapped = jnp.broadcast_to(table[:, 2 * MAX_REL:], (nh, ATT_BQ))
    ruler = jnp.concatenate([far_past, table[:, ::-1], far_future, wrapped], axis=1)
    return ruler.astype(F32).reshape(nh, 1, BIAS_RULER)


def kernel(x_prompt, x_sample, cache_a_k, cache_a_v, cache_b_k, cache_b_v, ffn1_norm, ffn1_w_gate, ffn1_w_up, ffn1_w_down, mix_norm, w_in, lambda_q1, lambda_k1, lambda_q2, lambda_k2, subln_a, rel_bias_b, w_branch_a, w_branch_b, w_out, ffn2_norm, ffn2_w_gate, ffn2_w_up, ffn2_w_down, final_norm):
    batch, seq, _ = x_prompt.shape
    dec_batch, dec_seq, _ = x_sample.shape
    depth = ffn1_norm.shape[0]
    past_a = cache_a_k.shape[2]
    past_b = cache_b_k.shape[2]
    assert seq % max(ATT_BQ, PROJ_BM) == 0 and seq >= BAND_ROWS
    assert dec_seq == CHUNK and past_b == BAND_ROWS and PROJ_BM % dec_seq == 0

    xp = x_prompt.reshape(batch * seq, D_MODEL)
    xs = x_sample.reshape(dec_batch * dec_seq, D_MODEL)
    rope_p = _rope_tables(jnp.arange(seq))
    rope_s = _rope_tables(past_a + (jnp.arange(PROJ_BM) % dec_seq))
    cache_a_k2 = cache_a_k.reshape(depth, dec_batch, past_a * H_A, HEAD_LANES)
    cache_a_v2 = cache_a_v.reshape(depth, dec_batch, past_a * H_A, HEAD_LANES)
    cache_b_k2 = cache_b_k.reshape(depth, dec_batch, past_b * H_B, HEAD_LANES)
    cache_b_v2 = cache_b_v.reshape(depth, dec_batch, past_b * H_B, HEAD_LANES)

    outs = {k: [] for k in ("akp", "avp", "bkp", "bvp", "aks", "avs", "bks", "bvs")}
    for layer in range(depth):
        lam_init = 0.8 - 0.6 * math.exp(-0.3 * layer)
        lam = (jnp.exp(jnp.sum(lambda_q1[layer].astype(F32) * lambda_k1[layer].astype(F32)))
               - jnp.exp(jnp.sum(lambda_q2[layer].astype(F32) * lambda_k2[layer].astype(F32)))
               + lam_init).reshape(1)
        lam_scale = 1.0 - lam_init
        g1 = ffn1_norm[layer].reshape(1, D_MODEL)
        gm = mix_norm[layer].reshape(1, D_MODEL)
        g2 = ffn2_norm[layer].reshape(1, D_MODEL)
        gf = final_norm.reshape(1, D_MODEL)
        subln = subln_a[layer].reshape(1, DV_A)
        w1g, w1u, w1d = (w[layer].astype(BF16) for w in (ffn1_w_gate, ffn1_w_up, ffn1_w_down))
        w2g, w2u, w2d = (w[layer].astype(BF16) for w in (ffn2_w_gate, ffn2_w_up, ffn2_w_down))
        win = w_in[layer].astype(BF16)
        wa, wb, wo = (w[layer].astype(BF16) for w in (w_branch_a, w_branch_b, w_out))
        table = rel_bias_b[layer]
        ruler = _bias_ruler(table)
        last = layer == depth - 1

        def trunk(x, rope_tabs, rope_period, attend, transposed_a):
            x1, hn = _ffn(x, g1, w1g, w1u, w1d, gm, "mix")
            c = 0
            qa = _proj(hn, win, c, W_QK_A, "rope_q", rope_tabs, rope_period, transposed=transposed_a); c += W_QK_A
            ka, ka16 = _proj(hn, win, c, W_QK_A, "rope_kv", rope_tabs, rope_period); c += W_QK_A
            va, va16 = _proj(hn, win, c, W_A, "kv", transposed=transposed_a); c += W_A
            qb = _proj(hn, win, c, W_B, "plain"); c += W_B
            kb, kb16 = _proj(hn, win, c, W_B, "kv"); c += W_B
            vb, vb16 = _proj(hn, win, c, W_B, "kv"); c += W_B
            sga = _proj(hn, win, c, D_MODEL, "sigmoid"); c += D_MODEL
            sgb = _proj(hn, win, c, D_MODEL, "sigmoid")
            oa, ob = attend(qa, ka16, va16, qb, kb16, vb16)
            x2 = _merge(x1, oa, ob, sga, sgb, wa, wb, wo)
            if last:
                xo = _ffn(x2, g2, w2g, w2u, w2d, gf, "final")
            else:
                xo = _ffn(x2, g2, w2g, w2u, w2d, gf, "mix")[0]
            return xo, ka, va, kb, vb

        def attend_prompt(qa, ka16, va16, qb, kb16, vb16):
            oa = _attn_a_prompt(lam, qa, ka16, va16, subln, batch, seq, lam_scale)
            ob = _attn_b_prompt(qb, kb16, vb16, ruler, batch, seq)
            return oa, ob

        def attend_sample(qa, ka16, va16, qb, kb16, vb16):
            oa = _attn_a_sample(lam, qa, cache_a_k2, cache_a_v2, ka16, va16, subln, layer, lam_scale)
            ob = _attn_b_sample(qb, cache_b_k2, cache_b_v2, kb16, vb16, ruler, layer)
            return oa, ob

        xp, ka, va, kb, vb = trunk(xp, rope_p, seq, attend_prompt, True)
        outs["akp"].append(ka.reshape(batch, seq, H_A, 2 * DK_A))
        outs["avp"].append(va.reshape(batch, seq, H_A, DV_A))
        rows = min(BAND_ROWS, seq)
        outs["bkp"].append(kb.reshape(batch, seq, H_B, DH_B)[:, seq - rows:])
        outs["bvp"].append(vb.reshape(batch, seq, H_B, DH_B)[:, seq - rows:])

        xs, ka, va, kb, vb = trunk(xs, rope_s, PROJ_BM, attend_sample, False)
        outs["aks"].append(ka.reshape(dec_batch, dec_seq, H_A, 2 * DK_A))
        outs["avs"].append(va.reshape(dec_batch, dec_seq, H_A, DV_A))
        kb_all = jnp.concatenate([cache_b_k[layer], kb.reshape(dec_batch, dec_seq, H_B, DH_B)], axis=1)
        vb_all = jnp.concatenate([cache_b_v[layer], vb.reshape(dec_batch, dec_seq, H_B, DH_B)], axis=1)
        outs["bks"].append(kb_all[:, dec_seq:])
        outs["bvs"].append(vb_all[:, dec_seq:])

    y_prompt = xp.reshape(batch, seq, D_MODEL)
    y_sample = xs.reshape(dec_batch, dec_seq, D_MODEL)
    return (y_prompt, y_sample,
            jnp.stack(outs["akp"]), jnp.stack(outs["avp"]), jnp.stack(outs["bkp"]), jnp.stack(outs["bvp"]),
            jnp.stack(outs["aks"]), jnp.stack(outs["avs"]), jnp.stack(outs["bks"]), jnp.stack(outs["bvs"]))
```

```python
import functools
import math

import jax
import jax.numpy as jnp
from jax import lax
from jax.experimental import pallas as pl
from jax.experimental.pallas import tpu as pltpu

F32 = jnp.float32
BF16 = jnp.bfloat16

D_MODEL = 2048
CHUNK = 64
H_A = 8
DK_A = 64
DV_A = 2 * DK_A
W_QK_A = H_A * 2 * DK_A
W_A = H_A * DV_A
H_B = 8
DH_B = 128
W_B = H_B * DH_B
N_PREV = 8
BAND_ROWS = N_PREV * CHUNK
MAX_REL = 128
FFN_DIM = 4 * D_MODEL
ROPE_THETA = 10000.0
EPS = 1e-6
NEG = -1e30

HEAD_LANES = 128
VMEM_LIMIT = 58 * 1024 * 1024

FFN_BM = 512
FFN_BF = 1024
PROJ_BM = 1024
PROJ_BN = 1024
MERGE_BM = 256
ATT_BQ = 256
BAND_HEADS_PER_STEP = 4
BAND_WIN = BAND_ROWS + ATT_BQ
BIAS_RULER = BAND_WIN + ATT_BQ


def _params(*sem):
    return pltpu.CompilerParams(dimension_semantics=sem, vmem_limit_bytes=VMEM_LIMIT)


def _rms(x, g):
    return x * lax.rsqrt(jnp.mean(x * x, axis=-1, keepdims=True) + EPS) * g


def _ffn_kernel(x_ref, g_ref, wg_ref, wu_ref, wd_ref, g2_ref, *rest, mode):
    if mode == "mix":
        o_ref, hn_ref, h_ref = rest
    else:
        o_ref, h_ref = rest
    f = pl.program_id(1)

    @pl.when(f == 0)
    def _():
        x = x_ref[...]
        h_ref[...] = _rms(x, g_ref[...]).astype(BF16)
        o_ref[...] = x

    h = h_ref[...]
    a = jnp.dot(h, wg_ref[...], preferred_element_type=F32)
    b = jnp.dot(h, wu_ref[...], preferred_element_type=F32)
    t = (0.5 * (a * jax.nn.sigmoid(a)) * b).astype(BF16)
    o_ref[...] += jnp.dot(t, wd_ref[...], preferred_element_type=F32)

    @pl.when(f == pl.num_programs(1) - 1)
    def _():
        y = _rms(o_ref[...], g2_ref[...])
        if mode == "mix":
            hn_ref[...] = y.astype(BF16)
        else:
            o_ref[...] = y


def _ffn(x, g, wg, wu, wd, g2, mode):
    m = x.shape[0]
    bm, bf = FFN_BM, FFN_BF
    grid = (m // bm, FFN_DIM // bf)
    row = pl.BlockSpec((bm, D_MODEL), lambda i, f: (i, 0))
    vec = pl.BlockSpec((1, D_MODEL), lambda i, f: (0, 0))
    in_specs = [
        row, vec,
        pl.BlockSpec((D_MODEL, bf), lambda i, f: (0, f)),
        pl.BlockSpec((D_MODEL, bf), lambda i, f: (0, f)),
        pl.BlockSpec((bf, D_MODEL), lambda i, f: (f, 0)),
        vec,
    ]
    if mode == "mix":
        out_shape = (jax.ShapeDtypeStruct((m, D_MODEL), F32), jax.ShapeDtypeStruct((m, D_MODEL), BF16))
        out_specs = (row, row)
    else:
        out_shape = jax.ShapeDtypeStruct((m, D_MODEL), F32)
        out_specs = row
    return pl.pallas_call(
        functools.partial(_ffn_kernel, mode=mode),
        grid=grid, in_specs=in_specs, out_specs=out_specs, out_shape=out_shape,
        scratch_shapes=[pltpu.VMEM((bm, D_MODEL), BF16)],
        compiler_params=_params("parallel", "arbitrary"),
        name="ffn_" + mode,
    )(x, g, wg, wu, wd, g2)


def _rope_slabs(z, cos, sin_signed):
    lane = lax.broadcasted_iota(jnp.int32, cos.shape, 1)
    first_half = (lane % DK_A) < (DK_A // 2)
    outs = []
    for s in range(z.shape[1] // HEAD_LANES):
        slab = z[:, s * HEAD_LANES:(s + 1) * HEAD_LANES]
        partner = jnp.where(first_half,
                            pltpu.roll(slab, HEAD_LANES - DK_A // 2, 1),
                            pltpu.roll(slab, DK_A // 2, 1))
        outs.append(slab * cos + partner * sin_signed)
    return jnp.concatenate(outs, axis=1)


def _proj_kernel(h_ref, w_ref, *rest, epilogue, transposed):
    rope = epilogue in ("rope_q", "rope_kv")
    heads = epilogue in ("rope_kv", "kv")
    if rope:
        cos_ref, sin_ref, *rest = rest
    z = jnp.dot(h_ref[...], w_ref[...], preferred_element_type=F32)
    if rope:
        z = _rope_slabs(z, cos_ref[...], sin_ref[...])
    if epilogue == "rope_q":
        z = z * (DK_A ** -0.5)
    elif epilogue == "sigmoid":
        z = jax.nn.sigmoid(z)
    nh = z.shape[1] // HEAD_LANES
    if heads:
        o_ref, o16_ref = rest
        for h in range(nh):
            zh = z[:, h * HEAD_LANES:(h + 1) * HEAD_LANES]
            o_ref[pl.ds(h, z.shape[0], stride=nh), :] = zh
            o16_ref[h] = (zh.T if transposed else zh).astype(BF16)
    elif transposed:
        (o_ref,) = rest
        for h in range(nh):
            o_ref[h] = z[:, h * HEAD_LANES:(h + 1) * HEAD_LANES].T.astype(BF16)
    else:
        (o_ref,) = rest
        o_ref[...] = z.astype(o_ref.dtype)


def _proj(hn, w_in, col0, ncols, epilogue, rope_tabs=None, rope_period=None, transposed=False):
    m = hn.shape[0]
    bm, bn = PROJ_BM, PROJ_BN
    joff = col0 // bn
    nh = bn // HEAD_LANES
    head_major = (pl.BlockSpec((nh, HEAD_LANES, bm), lambda i, j: (0, 0, i)) if transposed
                  else pl.BlockSpec((nh, bm, HEAD_LANES), lambda i, j: (0, i, 0)))
    head_major_shape = jax.ShapeDtypeStruct((nh, HEAD_LANES, m) if transposed else (nh, m, HEAD_LANES), BF16)
    in_specs = [
        pl.BlockSpec((bm, D_MODEL), lambda i, j: (i, 0)),
        pl.BlockSpec((D_MODEL, bn), lambda i, j: (0, j + joff)),
    ]
    args = [hn, w_in]
    if rope_tabs is not None:
        nper = rope_period // bm
        tab = pl.BlockSpec((bm, HEAD_LANES), lambda i, j: (i % nper, 0))
        in_specs += [tab, tab]
        args += list(rope_tabs)
    if epilogue in ("rope_kv", "kv"):
        assert ncols == bn
        out_specs = (pl.BlockSpec((bm * nh, HEAD_LANES), lambda i, j: (i, 0)), head_major)
        out_shape = (jax.ShapeDtypeStruct((m * nh, HEAD_LANES), F32), head_major_shape)
    elif transposed:
        assert ncols == bn
        out_specs, out_shape = head_major, head_major_shape
    else:
        out_specs = pl.BlockSpec((bm, bn), lambda i, j: (i, j))
        out_shape = jax.ShapeDtypeStruct((m, ncols), BF16)
    return pl.pallas_call(
        functools.partial(_proj_kernel, epilogue=epilogue, transposed=transposed),
        grid=(m // bm, ncols // bn), in_specs=in_specs,
        out_specs=out_specs, out_shape=out_shape,
        compiler_params=_params("parallel", "arbitrary"),
        name="proj_" + epilogue,
    )(*args)


def _block_diag_q(q):
    lane = lax.broadcasted_iota(jnp.int32, q.shape, 1)
    zero = jnp.zeros_like(q)
    return jnp.concatenate([jnp.where(lane < DK_A, q, zero), jnp.where(lane >= DK_A, q, zero)], axis=0)


def _qk(q, k):
    return lax.dot_general(q, k, (((1,), (1,)), ((), ())), preferred_element_type=F32)


def _diff_finish(acc, l, lam, subln, n):
    o = acc / l
    o = o[:n] - lam * o[n:]
    return _rms(o, subln)


def _attn_a_prompt_kernel(lam_ref, qt_ref, k_ref, vt_ref, subln_ref, o_ref, *, lam_scale):
    qi = pl.program_id(1)
    nh, _, bq = qt_ref.shape
    row = lax.broadcasted_iota(jnp.int32, (HEAD_LANES, bq), 0)
    qbd = []
    for h in range(nh):
        qt = qt_ref[h]
        zero = jnp.zeros_like(qt)
        qbd.append(jnp.concatenate([jnp.where(row < DK_A, qt, zero), jnp.where(row >= DK_A, qt, zero)], axis=1))

    def step(j, carries, masked):
        start = pl.multiple_of(j * bq, bq)
        scores = [jnp.dot(k_ref[h, pl.ds(start, bq), :], qbd[h], preferred_element_type=F32) for h in range(nh)]
        out = []
        for h in range(nh):
            m, l, acc = carries[h]
            s = scores[h]
            if masked:
                kr = lax.broadcasted_iota(jnp.int32, s.shape, 0)
                qc = lax.broadcasted_iota(jnp.int32, s.shape, 1) % bq
                s = jnp.where((kr // CHUNK) <= (qc // CHUNK), s, NEG)
            m_new = jnp.maximum(m, jnp.max(s, axis=0, keepdims=True))
            alpha = jnp.exp(m - m_new)
            p = jnp.exp(s - m_new)
            l = alpha * l + jnp.sum(p, axis=0, keepdims=True)
            acc = alpha * acc + jnp.dot(vt_ref[h, :, pl.ds(start, bq)], p.astype(BF16),
                                        preferred_element_type=F32)
            out.append((m_new, l, acc))
        return tuple(out)

    init = (jnp.full((1, 2 * bq), -jnp.inf, F32), jnp.zeros((1, 2 * bq), F32),
            jnp.zeros((HEAD_LANES, 2 * bq), F32))
    carries = lax.fori_loop(0, qi, functools.partial(step, masked=False), (init,) * nh)
    carries = step(qi, carries, masked=True)
    for h in range(nh):
        _, l, acc = carries[h]
        o = acc / l
        o = (o[:, :bq] - lam_ref[0] * o[:, bq:]).T
        o = _rms(o, subln_ref[...]) * lam_scale
        o_ref[:, h * HEAD_LANES:(h + 1) * HEAD_LANES] = o.astype(o_ref.dtype)


def _attn_a_prompt(lam, qt, k16, vt, subln, batch, seq, lam_scale):
    bq = ATT_BQ
    nq = seq // bq
    return pl.pallas_call(
        functools.partial(_attn_a_prompt_kernel, lam_scale=lam_scale),
        grid=(batch, nq),
        in_specs=[
            pl.BlockSpec(memory_space=pltpu.SMEM),
            pl.BlockSpec((H_A, HEAD_LANES, bq), lambda b, i: (0, 0, b * nq + i)),
            pl.BlockSpec((H_A, seq, HEAD_LANES), lambda b, i: (0, b, 0)),
            pl.BlockSpec((H_A, HEAD_LANES, seq), lambda b, i: (0, 0, b)),
            pl.BlockSpec((1, DV_A), lambda b, i: (0, 0)),
        ],
        out_specs=pl.BlockSpec((bq, W_A), lambda b, i: (b * nq + i, 0)),
        out_shape=jax.ShapeDtypeStruct((batch * seq, W_A), BF16),
        compiler_params=_params("parallel", "arbitrary"),
        name="attn_a_prompt",
    )(lam, qt, k16, vt, subln)


def _softmax_two_parts(sc, sn, vc, vn):
    m = jnp.maximum(jnp.max(sc, axis=-1, keepdims=True), jnp.max(sn, axis=-1, keepdims=True))
    pc = jnp.exp(sc - m)
    pn = jnp.exp(sn - m)
    l = jnp.sum(pc, axis=-1, keepdims=True) + jnp.sum(pn, axis=-1, keepdims=True)
    acc = (jnp.dot(pc.astype(BF16), vc, preferred_element_type=F32)
           + jnp.dot(pn.astype(BF16), vn, preferred_element_type=F32))
    return acc, l


def _attn_a_sample_kernel(lam_ref, q_ref, kc_ref, vc_ref, kn_ref, vn_ref, subln_ref, o_ref, *, lam_scale):
    n = q_ref.shape[0]
    past = kc_ref.shape[0] // H_A
    for h in range(H_A):
        cols = slice(h * HEAD_LANES, (h + 1) * HEAD_LANES)
        head_rows = pl.ds(h, past, stride=H_A)
        qbd = _block_diag_q(q_ref[:, cols])
        sc = _qk(qbd, kc_ref[head_rows, :].astype(BF16))
        sn = _qk(qbd, kn_ref[h])
        acc, l = _softmax_two_parts(sc, sn, vc_ref[head_rows, :].astype(BF16), vn_ref[h])
        o = _diff_finish(acc, l, lam_ref[0], subln_ref[...], n) * lam_scale
        o_ref[:, cols] = o.astype(o_ref.dtype)


def _attn_a_sample(lam, qa, cache_k, cache_v, k16, v16, subln, layer, lam_scale):
    _, nb, cache_rows, _ = cache_k.shape
    n = qa.shape[0] // nb
    rows = pl.BlockSpec((n, W_A), lambda b: (b, 0))
    new = pl.BlockSpec((H_A, n, HEAD_LANES), lambda b: (0, b, 0))
    old = pl.BlockSpec((None, None, cache_rows, HEAD_LANES), lambda b: (layer, b, 0, 0))
    return pl.pallas_call(
        functools.partial(_attn_a_sample_kernel, lam_scale=lam_scale),
        grid=(nb,),
        in_specs=[pl.BlockSpec(memory_space=pltpu.SMEM), rows, old, old, new, new,
                  pl.BlockSpec((1, DV_A), lambda b: (0, 0))],
        out_specs=rows,
        out_shape=jax.ShapeDtypeStruct(qa.shape, BF16),
        compiler_params=_params("parallel"),
        name="attn_a_sample",
    )(lam, qa, cache_k, cache_v, k16, v16, subln)


def _toeplitz_bias(ruler_row, nq, nk):
    rows = jnp.broadcast_to(ruler_row, (nq, ruler_row.shape[1]))
    return pltpu.roll(rows, 0, 1, stride=1, stride_axis=0)[:, :nk]


def _attn_b_prompt_kernel(q_ref, k_ref, v_ref, ruler_ref, o_ref, kb_ref, vb_ref, bias_ref):
    qi = pl.program_id(2)
    bq = q_ref.shape[0]
    ng, seq, _ = k_ref.shape

    @pl.when(qi == 0)
    def _():
        zeros = jnp.zeros((BAND_ROWS, HEAD_LANES), BF16)
        d = (lax.broadcasted_iota(jnp.int32, (bq, BAND_WIN), 1) // CHUNK
             - lax.broadcasted_iota(jnp.int32, (bq, BAND_WIN), 0) // CHUNK)
        for g in range(ng):
            kb_ref[g, 0:BAND_ROWS, :] = zeros
            vb_ref[g, 0:BAND_ROWS, :] = zeros
            kb_ref[g, BAND_ROWS:BAND_ROWS + seq, :] = k_ref[g]
            vb_ref[g, BAND_ROWS:BAND_ROWS + seq, :] = v_ref[g]
            bias = _toeplitz_bias(ruler_ref[g], bq, BAND_WIN)
            bias_ref[g] = jnp.where(d >= 0, jnp.where(d <= N_PREV, bias, NEG), NEG)

    start = pl.multiple_of(qi * bq, bq)
    win = pl.ds(start, BAND_WIN)
    scores = [_qk(q_ref[:, g * HEAD_LANES:(g + 1) * HEAD_LANES], kb_ref[g, win, :]) for g in range(ng)]
    w = lax.broadcasted_iota(jnp.int32, (bq, BAND_WIN), 1)
    in_sequence = w + qi * bq >= BAND_ROWS
    for g in range(ng):
        s = scores[g] * (DH_B ** -0.5) + bias_ref[g]
        s = jnp.where(in_sequence, s, NEG)
        p = jnp.exp(s - jnp.max(s, axis=-1, keepdims=True))
        l = jnp.sum(p, axis=-1, keepdims=True)
        o = jnp.dot(p.astype(BF16), vb_ref[g, win, :], preferred_element_type=F32) / l
        o_ref[:, g * HEAD_LANES:(g + 1) * HEAD_LANES] = o.astype(o_ref.dtype)


def _attn_b_prompt(qb, k16, v16, ruler, batch, seq):
    bq, ng = ATT_BQ, BAND_HEADS_PER_STEP
    nq = seq // bq
    kv = pl.BlockSpec((ng, seq, HEAD_LANES), lambda b, h, i: (h, b, 0))
    rows = pl.BlockSpec((bq, ng * HEAD_LANES), lambda b, h, i: (b * nq + i, h))
    return pl.pallas_call(
        _attn_b_prompt_kernel,
        grid=(batch, H_B // ng, nq),
        in_specs=[rows, kv, kv, pl.BlockSpec((ng, 1, BIAS_RULER), lambda b, h, i: (h, 0, 0))],
        out_specs=rows,
        out_shape=jax.ShapeDtypeStruct((batch * seq, W_B), BF16),
        scratch_shapes=[pltpu.VMEM((ng, BAND_ROWS + seq, HEAD_LANES), BF16),
                        pltpu.VMEM((ng, BAND_ROWS + seq, HEAD_LANES), BF16),
                        pltpu.VMEM((ng, bq, BAND_WIN), F32)],
        compiler_params=_params("parallel", "parallel", "arbitrary"),
        name="attn_b_prompt",
    )(qb, k16, v16, ruler)


def _attn_b_sample_kernel(q_ref, kc_ref, vc_ref, kn_ref, vn_ref, ruler_ref, o_ref, bias_ref):
    n = q_ref.shape[0]
    past = kc_ref.shape[0] // H_B
    scale = DH_B ** -0.5

    @pl.when(pl.program_id(0) == 0)
    def _():
        for h in range(H_B):
            bias_ref[h] = _toeplitz_bias(ruler_ref[h], n, past + n)

    for h in range(H_B):
        cols = slice(h * HEAD_LANES, (h + 1) * HEAD_LANES)
        head_rows = pl.ds(h, past, stride=H_B)
        q = q_ref[:, cols]
        sc = _qk(q, kc_ref[head_rows, :].astype(BF16)) * scale + bias_ref[h, :, :past]
        sn = _qk(q, kn_ref[h]) * scale + bias_ref[h, :, past:]
        acc, l = _softmax_two_parts(sc, sn, vc_ref[head_rows, :].astype(BF16), vn_ref[h])
        o_ref[:, cols] = (acc / l).astype(o_ref.dtype)


def _attn_b_sample(qb, cache_k, cache_v, k16, v16, ruler, layer):
    _, nb, cache_rows, _ = cache_k.shape
    past = cache_rows // H_B
    n = qb.shape[0] // nb
    rows = pl.BlockSpec((n, W_B), lambda b: (b, 0))
    new = pl.BlockSpec((H_B, n, HEAD_LANES), lambda b: (0, b, 0))
    old = pl.BlockSpec((None, None, cache_rows, HEAD_LANES), lambda b: (layer, b, 0, 0))
    return pl.pallas_call(
        _attn_b_sample_kernel,
        grid=(nb,),
        in_specs=[rows, old, old, new, new,
                  pl.BlockSpec((H_B, 1, BIAS_RULER), lambda b: (0, 0, 0))],
        out_specs=rows,
        out_shape=jax.ShapeDtypeStruct(qb.shape, BF16),
        scratch_shapes=[pltpu.VMEM((H_B, n, past + n), F32)],
        compiler_params=_params("arbitrary"),
        name="attn_b_sample",
    )(qb, cache_k, cache_v, k16, v16, ruler)


def _merge_kernel(x_ref, oa_ref, ob_ref, sga_ref, sgb_ref, wa_ref, wb_ref, wo_ref, o_ref):
    a = jnp.dot(oa_ref[...], wa_ref[...], preferred_element_type=F32)
    b = jnp.dot(ob_ref[...], wb_ref[...], preferred_element_type=F32)
    merged = sga_ref[...].astype(F32) * a + sgb_ref[...].astype(F32) * b
    o_ref[...] = x_ref[...] + jnp.dot(merged.astype(BF16), wo_ref[...], preferred_element_type=F32)


def _merge(x, oa, ob, sga, sgb, wa, wb, wo):
    m = x.shape[0]
    bm = MERGE_BM
    wide = pl.BlockSpec((bm, D_MODEL), lambda i: (i, 0))
    half = pl.BlockSpec((bm, W_A), lambda i: (i, 0))

    def whole(shape):
        return pl.BlockSpec(shape, lambda i: (0, 0))

    return pl.pallas_call(
        _merge_kernel,
        grid=(m // bm,),
        in_specs=[wide, half, half, wide, wide,
                  whole((W_A, D_MODEL)), whole((W_B, D_MODEL)), whole((D_MODEL, D_MODEL))],
        out_specs=wide,
        out_shape=jax.ShapeDtypeStruct((m, D_MODEL), F32),
        compiler_params=_params("parallel"),
        name="merge",
    )(x, oa, ob, sga, sgb, wa, wb, wo)


def _rope_tables(pos):
    half = DK_A // 2
    inv = ROPE_THETA ** (-jnp.arange(half, dtype=F32) / half)
    ang = pos.astype(F32)[:, None] * inv[None, :]
    cos, sin = jnp.cos(ang), jnp.sin(ang)
    reps = HEAD_LANES // DK_A
    cos_t = jnp.tile(jnp.concatenate([cos, cos], axis=1), (1, reps))
    sin_t = jnp.tile(jnp.concatenate([-sin, sin], axis=1), (1, reps))
    return cos_t, sin_t


def _bias_ruler(table):
    nh = table.shape[0]
    far_past = jnp.broadcast_to(table[:, 2 * MAX_REL:], (nh, BAND_ROWS - MAX_REL))
    far_future = jnp.broadcast_to(table[:, :1], (nh, BIAS_RULER - ATT_BQ - BAND_ROWS - MAX_REL - 1))
    wrapped = jnp.broadcast_to(table[:, 2 * MAX_REL:], (nh, ATT_BQ))
    ruler = jnp.concatenate([far_past, table[:, ::-1], far_future, wrapped], axis=1)
    return ruler.astype(F32).reshape(nh, 1, BIAS_RULER)


def kernel(x_prompt, x_sample, cache_a_k, cache_a_v, cache_b_k, cache_b_v, ffn1_norm, ffn1_w_gate, ffn1_w_up, ffn1_w_down, mix_norm, w_in, lambda_q1, lambda_k1, lambda_q2, lambda_k2, subln_a, rel_bias_b, w_branch_a, w_branch_b, w_out, ffn2_norm, ffn2_w_gate, ffn2_w_up, ffn2_w_down, final_norm):
    batch, seq, _ = x_prompt.shape
    dec_batch, dec_seq, _ = x_sample.shape
    depth = ffn1_norm.shape[0]
    past_a = cache_a_k.shape[2]
    past_b = cache_b_k.shape[2]
    assert seq % max(ATT_BQ, PROJ_BM) == 0 and seq >= BAND_ROWS
    assert dec_seq == CHUNK and past_b == BAND_ROWS and PROJ_BM % dec_seq == 0

    xp = x_prompt.reshape(batch * seq, D_MODEL)
    xs = x_sample.reshape(dec_batch * dec_seq, D_MODEL)
    rope_p = _rope_tables(jnp.arange(seq))
    rope_s = _rope_tables(past_a + (jnp.arange(PROJ_BM) % dec_seq))
    cache_a_k2 = cache_a_k.reshape(depth, dec_batch, past_a * H_A, HEAD_LANES)
    cache_a_v2 = cache_a_v.reshape(depth, dec_batch, past_a * H_A, HEAD_LANES)
    cache_b_k2 = cache_b_k.reshape(depth, dec_batch, past_b * H_B, HEAD_LANES)
    cache_b_v2 = cache_b_v.reshape(depth, dec_batch, past_b * H_B, HEAD_LANES)

    outs = {k: [] for k in ("akp", "avp", "bkp", "bvp", "aks", "avs", "bks", "bvs")}
    for layer in range(depth):
        lam_init = 0.8 - 0.6 * math.exp(-0.3 * layer)
        lam = (jnp.exp(jnp.sum(lambda_q1[layer].astype(F32) * lambda_k1[layer].astype(F32)))
               - jnp.exp(jnp.sum(lambda_q2[layer].astype(F32) * lambda_k2[layer].astype(F32)))
               + lam_init).reshape(1)
        lam_scale = 1.0 - lam_init
        g1 = ffn1_norm[layer].reshape(1, D_MODEL)
        gm = mix_norm[layer].reshape(1, D_MODEL)
        g2 = ffn2_norm[layer].reshape(1, D_MODEL)
        gf = final_norm.reshape(1, D_MODEL)
        subln = subln_a[layer].reshape(1, DV_A)
        w1g, w1u, w1d = (w[layer].astype(BF16) for w in (ffn1_w_gate, ffn1_w_up, ffn1_w_down))
        w2g, w2u, w2d = (w[layer].astype(BF16) for w in (ffn2_w_gate, ffn2_w_up, ffn2_w_down))
        win = w_in[layer].astype(BF16)
        wa, wb, wo = (w[layer].astype(BF16) for w in (w_branch_a, w_branch_b, w_out))
        table = rel_bias_b[layer]
        ruler = _bias_ruler(table)
        last = layer == depth - 1

        def trunk(x, rope_tabs, rope_period, attend, transposed_a):
            x1, hn = _ffn(x, g1, w1g, w1u, w1d, gm, "mix")
            c = 0
            qa = _proj(hn, win, c, W_QK_A, "rope_q", rope_tabs, rope_period, transposed=transposed_a); c += W_QK_A
            ka, ka16 = _proj(hn, win, c, W_QK_A, "rope_kv", rope_tabs, rope_period); c += W_QK_A
            va, va16 = _proj(hn, win, c, W_A, "kv", transposed=transposed_a); c += W_A
            qb = _proj(hn, win, c, W_B, "plain"); c += W_B
            kb, kb16 = _proj(hn, win, c, W_B, "kv"); c += W_B
            vb, vb16 = _proj(hn, win, c, W_B, "kv"); c += W_B
            sga = _proj(hn, win, c, D_MODEL, "sigmoid"); c += D_MODEL
            sgb = _proj(hn, win, c, D_MODEL, "sigmoid")
            oa, ob = attend(qa, ka16, va16, qb, kb16, vb16)
            x2 = _merge(x1, oa, ob, sga, sgb, wa, wb, wo)
            if last:
                xo = _ffn(x2, g2, w2g, w2u, w2d, gf, "final")
            else:
                xo = _ffn(x2, g2, w2g, w2u, w2d, gf, "mix")[0]
            return xo, ka, va, kb, vb

        def attend_prompt(qa, ka16, va16, qb, kb16, vb16):
            oa = _attn_a_prompt(lam, qa, ka16, va16, subln, batch, seq, lam_scale)
            ob = _attn_b_prompt(qb, kb16, vb16, ruler, batch, seq)
            return oa, ob

        def attend_sample(qa, ka16, va16, qb, kb16, vb16):
            oa = _attn_a_sample(lam, qa, cache_a_k2, cache_a_v2, ka16, va16, subln, layer, lam_scale)
            ob = _attn_b_sample(qb, cache_b_k2, cache_b_v2, kb16, vb16, ruler, layer)
            return oa, ob

        xp, ka, va, kb, vb = trunk(xp, rope_p, seq, attend_prompt, True)
        outs["akp"].append(ka.reshape(batch, seq, H_A, 2 * DK_A))
        outs["avp"].append(va.reshape(batch, seq, H_A, DV_A))
        rows = min(BAND_ROWS, seq)
        outs["bkp"].append(kb.reshape(batch, seq, H_B, DH_B)[:, seq - rows:])
        outs["bvp"].append(vb.reshape(batch, seq, H_B, DH_B)[:, seq - rows:])

        xs, ka, va, kb, vb = trunk(xs, rope_s, PROJ_BM, attend_sample, False)
        outs["aks"].append(ka.reshape(dec_batch, dec_seq, H_A, 2 * DK_A))
        outs["avs"].append(va.reshape(dec_batch, dec_seq, H_A, DV_A))
        kb_all = jnp.concatenate([cache_b_k[layer], kb.reshape(dec_batch, dec_seq, H_B, DH_B)], axis=1)
        vb_all = jnp.concatenate([cache_b_v[layer], vb.reshape(dec_batch, dec_seq, H_B, DH_B)], axis=1)
        outs["bks"].append(kb_all[:, dec_seq:])
        outs["bvs"].append(vb_all[:, dec_seq:])

    y_prompt = xp.reshape(batch, seq, D_MODEL)
    y_sample = xs.reshape(dec_batch, dec_seq, D_MODEL)
    return (y_prompt, y_sample,
            jnp.stack(outs["akp"]), jnp.stack(outs["avp"]), jnp.stack(outs["bkp"]), jnp.stack(outs["bvp"]),
            jnp.stack(outs["aks"]), jnp.stack(outs["avs"]), jnp.stack(outs["bks"]), jnp.stack(outs["bvs"]))
```

```python
import functools
import math

import jax
import jax.numpy as jnp
from jax import lax
from jax.experimental import pallas as pl
from jax.experimental.pallas import tpu as pltpu

F32 = jnp.float32
BF16 = jnp.bfloat16

D_MODEL = 2048
CHUNK = 64
H_A = 8
DK_A = 64
DV_A = 2 * DK_A
W_QK_A = H_A * 2 * DK_A
W_A = H_A * DV_A
H_B = 8
DH_B = 128
W_B = H_B * DH_B
N_PREV = 8
BAND_ROWS = N_PREV * CHUNK
MAX_REL = 128
FFN_DIM = 4 * D_MODEL
ROPE_THETA = 10000.0
EPS = 1e-6
NEG = -1e30

HEAD_LANES = 128
BF16_SUBLANES = 16
VMEM_LIMIT = 58 * 1024 * 1024

FFN_BM = 512
FFN_BF = 1024
PROJ_BM = 1024
PROJ_BN = 1024
MERGE_BM = 256
ATT_BQ = 256
BAND_HEADS_PER_STEP = 4
BAND_WIN = BAND_ROWS + ATT_BQ
BIAS_RULER = BAND_WIN + ATT_BQ


def _params(*sem):
    return pltpu.CompilerParams(dimension_semantics=sem, vmem_limit_bytes=VMEM_LIMIT)


def _rms(x, g):
    return x * lax.rsqrt(jnp.mean(x * x, axis=-1, keepdims=True) + EPS) * g


def _ffn_kernel(x_ref, g_ref, wg_ref, wu_ref, wd_ref, g2_ref, *rest, mode, ncast):
    cast_in, rest = rest[:ncast], rest[ncast:]
    if mode == "mix":
        o_ref, hn_ref, *rest = rest
    else:
        o_ref, *rest = rest
    cast_out, (h_ref,) = rest[:ncast], rest[ncast:]
    f = pl.program_id(1)

    for src, dst in zip(cast_in, cast_out):
        dst[...] = src[...].astype(BF16)

    @pl.when(f == 0)
    def _():
        x = x_ref[...]
        h_ref[...] = _rms(x, g_ref[...]).astype(BF16)
        o_ref[...] = x

    h = h_ref[...]
    a = jnp.dot(h, wg_ref[...], preferred_element_type=F32)
    b = jnp.dot(h, wu_ref[...], preferred_element_type=F32)
    t = (0.5 * (a * jax.nn.sigmoid(a)) * b).astype(BF16)
    o_ref[...] += jnp.dot(t, wd_ref[...], preferred_element_type=F32)

    @pl.when(f == pl.num_programs(1) - 1)
    def _():
        y = _rms(o_ref[...], g2_ref[...])
        if mode == "mix":
            hn_ref[...] = y.astype(BF16)
        else:
            o_ref[...] = y


def _ffn(x, g, wg, wu, wd, g2, mode, casts=()):
    m = x.shape[0]
    bm, bf = FFN_BM, FFN_BF
    grid = (m // bm, FFN_DIM // bf)
    nf = grid[1]
    row = pl.BlockSpec((bm, D_MODEL), lambda i, f: (i, 0))
    vec = pl.BlockSpec((1, D_MODEL), lambda i, f: (0, 0))
    in_specs = [
        row, vec,
        pl.BlockSpec((D_MODEL, bf), lambda i, f: (0, f)),
        pl.BlockSpec((D_MODEL, bf), lambda i, f: (0, f)),
        pl.BlockSpec((bf, D_MODEL), lambda i, f: (f, 0)),
        vec,
    ]
    if mode == "mix":
        out_shape = [jax.ShapeDtypeStruct((m, D_MODEL), F32), jax.ShapeDtypeStruct((m, D_MODEL), BF16)]
        out_specs = [row, row]
    else:
        out_shape = [jax.ShapeDtypeStruct((m, D_MODEL), F32)]
        out_specs = [row]
    nsteps = grid[0] * nf
    for w in casts:
        rows, cols = w.shape
        if rows % (nsteps * BF16_SUBLANES) == 0:
            block_rows, hold = rows // nsteps, 1
        else:
            block_rows, hold = BF16_SUBLANES, nsteps * BF16_SUBLANES // rows
            assert rows * hold == nsteps * BF16_SUBLANES
        spec = pl.BlockSpec((block_rows, cols), functools.partial(lambda i, f, hold: ((i * nf + f) // hold, 0), hold=hold))
        in_specs.append(spec)
        out_specs.append(spec)
        out_shape.append(jax.ShapeDtypeStruct(w.shape, BF16))
    return pl.pallas_call(
        functools.partial(_ffn_kernel, mode=mode, ncast=len(casts)),
        grid=grid, in_specs=in_specs, out_specs=out_specs, out_shape=out_shape,
        scratch_shapes=[pltpu.VMEM((bm, D_MODEL), BF16)],
        compiler_params=_params("parallel", "arbitrary"),
        name="ffn_" + mode,
    )(x, g, wg, wu, wd, g2, *casts)


def _rope_slabs(z, cos, sin_signed):
    lane = lax.broadcasted_iota(jnp.int32, cos.shape, 1)
    first_half = (lane % DK_A) < (DK_A // 2)
    outs = []
    for s in range(z.shape[1] // HEAD_LANES):
        slab = z[:, s * HEAD_LANES:(s + 1) * HEAD_LANES]
        partner = jnp.where(first_half,
                            pltpu.roll(slab, HEAD_LANES - DK_A // 2, 1),
                            pltpu.roll(slab, DK_A // 2, 1))
        outs.append(slab * cos + partner * sin_signed)
    return jnp.concatenate(outs, axis=1)


def _proj_kernel(h_ref, w_ref, *rest, epilogue, transposed):
    rope = epilogue in ("rope_q", "rope_kv")
    heads = epilogue in ("rope_kv", "kv")
    if rope:
        cos_ref, sin_ref, *rest = rest
    z = jnp.dot(h_ref[...], w_ref[...], preferred_element_type=F32)
    if rope:
        z = _rope_slabs(z, cos_ref[...], sin_ref[...])
    if epilogue == "rope_q":
        z = z * (DK_A ** -0.5)
    elif epilogue == "sigmoid":
        z = jax.nn.sigmoid(z)
    nh = z.shape[1] // HEAD_LANES
    if heads:
        o_ref, o16_ref = rest
        for h in range(nh):
            zh = z[:, h * HEAD_LANES:(h + 1) * HEAD_LANES]
            o_ref[pl.ds(h, z.shape[0], stride=nh), :] = zh
            o16_ref[h] = (zh.T if transposed else zh).astype(BF16)
    elif transposed:
        (o_ref,) = rest
        for h in range(nh):
            o_ref[h] = z[:, h * HEAD_LANES:(h + 1) * HEAD_LANES].T.astype(BF16)
    else:
        (o_ref,) = rest
        o_ref[...] = z.astype(o_ref.dtype)


def _proj(hn, w_in, col0, ncols, epilogue, rope_tabs=None, rope_period=None, transposed=False):
    m = hn.shape[0]
    bm, bn = PROJ_BM, PROJ_BN
    joff = col0 // bn
    nh = bn // HEAD_LANES
    head_major = (pl.BlockSpec((nh, HEAD_LANES, bm), lambda i, j: (0, 0, i)) if transposed
                  else pl.BlockSpec((nh, bm, HEAD_LANES), lambda i, j: (0, i, 0)))
    head_major_shape = jax.ShapeDtypeStruct((nh, HEAD_LANES, m) if transposed else (nh, m, HEAD_LANES), BF16)
    in_specs = [
        pl.BlockSpec((bm, D_MODEL), lambda i, j: (i, 0)),
        pl.BlockSpec((D_MODEL, bn), lambda i, j: (0, j + joff)),
    ]
    args = [hn, w_in]
    if rope_tabs is not None:
        nper = rope_period // bm
        tab = pl.BlockSpec((bm, HEAD_LANES), lambda i, j: (i % nper, 0))
        in_specs += [tab, tab]
        args += list(rope_tabs)
    if epilogue in ("rope_kv", "kv"):
        assert ncols == bn
        out_specs = (pl.BlockSpec((bm * nh, HEAD_LANES), lambda i, j: (i, 0)), head_major)
        out_shape = (jax.ShapeDtypeStruct((m * nh, HEAD_LANES), F32), head_major_shape)
    elif transposed:
        assert ncols == bn
        out_specs, out_shape = head_major, head_major_shape
    else:
        out_specs = pl.BlockSpec((bm, bn), lambda i, j: (i, j))
        out_shape = jax.ShapeDtypeStruct((m, ncols), BF16)
    return pl.pallas_call(
        functools.partial(_proj_kernel, epilogue=epilogue, transposed=transposed),
        grid=(m // bm, ncols // bn), in_specs=in_specs,
        out_specs=out_specs, out_shape=out_shape,
        compiler_params=_params("parallel", "arbitrary"),
        name="proj_" + epilogue,
    )(*args)


def _block_diag_q(q):
    lane = lax.broadcasted_iota(jnp.int32, q.shape, 1)
    zero = jnp.zeros_like(q)
    return jnp.concatenate([jnp.where(lane < DK_A, q, zero), jnp.where(lane >= DK_A, q, zero)], axis=0)


def _qk(q, k):
    return lax.dot_general(q, k, (((1,), (1,)), ((), ())), preferred_element_type=F32)


def _diff_finish(acc, l, lam, subln, n):
    o = acc / l
    o = o[:n] - lam * o[n:]
    return _rms(o, subln)


def _attn_a_prompt_kernel(lam_ref, qt_ref, k_ref, vt_ref, subln_ref, o_ref, *, lam_scale):
    qi = pl.program_id(1)
    nh, _, bq = qt_ref.shape
    row = lax.broadcasted_iota(jnp.int32, (HEAD_LANES, bq), 0)
    qbd = []
    for h in range(nh):
        qt = qt_ref[h]
        zero = jnp.zeros_like(qt)
        qbd.append(jnp.concatenate([jnp.where(row < DK_A, qt, zero), jnp.where(row >= DK_A, qt, zero)], axis=1))

    def step(j, carries, masked):
        start = pl.multiple_of(j * bq, bq)
        scores = [jnp.dot(k_ref[h, pl.ds(start, bq), :], qbd[h], preferred_element_type=F32) for h in range(nh)]
        if masked:
            kr = lax.broadcasted_iota(jnp.int32, (bq, 2 * bq), 0)
            qc = lax.broadcasted_iota(jnp.int32, (bq, 2 * bq), 1) % bq
            visible = (kr // CHUNK) <= (qc // CHUNK)
        out = []
        for h in range(nh):
            m, l, acc = carries[h]
            s = scores[h]
            if masked:
                s = jnp.where(visible, s, NEG)
            m_new = jnp.maximum(m, jnp.max(s, axis=0, keepdims=True))
            alpha = jnp.exp(m - m_new)
            p = jnp.exp(s - m_new)
            l = alpha * l + jnp.sum(p, axis=0, keepdims=True)
            acc = alpha * acc + jnp.dot(vt_ref[h, :, pl.ds(start, bq)], p.astype(BF16),
                                        preferred_element_type=F32)
            out.append((m_new, l, acc))
        return tuple(out)

    init = (jnp.full((1, 2 * bq), -jnp.inf, F32), jnp.zeros((1, 2 * bq), F32),
            jnp.zeros((HEAD_LANES, 2 * bq), F32))
    carries = lax.fori_loop(0, qi, functools.partial(step, masked=False), (init,) * nh)
    carries = step(qi, carries, masked=True)
    for h in range(nh):
        _, l, acc = carries[h]
        o = acc / l
        o = (o[:, :bq] - lam_ref[0] * o[:, bq:]).T
        o = _rms(o, subln_ref[...]) * lam_scale
        o_ref[:, h * HEAD_LANES:(h + 1) * HEAD_LANES] = o.astype(o_ref.dtype)


def _attn_a_prompt(lam, qt, k16, vt, subln, batch, seq, lam_scale):
    bq = ATT_BQ
    nq = seq // bq
    return pl.pallas_call(
        functools.partial(_attn_a_prompt_kernel, lam_scale=lam_scale),
        grid=(batch, nq),
        in_specs=[
            pl.BlockSpec(memory_space=pltpu.SMEM),
            pl.BlockSpec((H_A, HEAD_LANES, bq), lambda b, i: (0, 0, b * nq + i)),
            pl.BlockSpec((H_A, seq, HEAD_LANES), lambda b, i: (0, b, 0)),
            pl.BlockSpec((H_A, HEAD_LANES, seq), lambda b, i: (0, 0, b)),
            pl.BlockSpec((1, DV_A), lambda b, i: (0, 0)),
        ],
        out_specs=pl.BlockSpec((bq, W_A), lambda b, i: (b * nq + i, 0)),
        out_shape=jax.ShapeDtypeStruct((batch * seq, W_A), BF16),
        compiler_params=_params("parallel", "arbitrary"),
        name="attn_a_prompt",
    )(lam, qt, k16, vt, subln)


def _softmax_two_parts(sc, sn, vc, vn):
    m = jnp.maximum(jnp.max(sc, axis=-1, keepdims=True), jnp.max(sn, axis=-1, keepdims=True))
    pc = jnp.exp(sc - m)
    pn = jnp.exp(sn - m)
    l = jnp.sum(pc, axis=-1, keepdims=True) + jnp.sum(pn, axis=-1, keepdims=True)
    acc = (jnp.dot(pc.astype(BF16), vc, preferred_element_type=F32)
           + jnp.dot(pn.astype(BF16), vn, preferred_element_type=F32))
    return acc, l


def _attn_a_sample_kernel(lam_ref, q_ref, kc_ref, vc_ref, kn_ref, vn_ref, subln_ref, o_ref, *, lam_scale):
    n = q_ref.shape[0]
    past = kc_ref.shape[0] // H_A
    for h in range(H_A):
        cols = slice(h * HEAD_LANES, (h + 1) * HEAD_LANES)
        head_rows = pl.ds(h, past, stride=H_A)
        qbd = _block_diag_q(q_ref[:, cols])
        sc = _qk(qbd, kc_ref[head_rows, :].astype(BF16))
        sn = _qk(qbd, kn_ref[h])
        acc, l = _softmax_two_parts(sc, sn, vc_ref[head_rows, :].astype(BF16), vn_ref[h])
        o = _diff_finish(acc, l, lam_ref[0], subln_ref[...], n) * lam_scale
        o_ref[:, cols] = o.astype(o_ref.dtype)


def _attn_a_sample(lam, qa, cache_k, cache_v, k16, v16, subln, layer, lam_scale):
    _, nb, cache_rows, _ = cache_k.shape
    n = qa.shape[0] // nb
    rows = pl.BlockSpec((n, W_A), lambda b: (b, 0))
    new = pl.BlockSpec((H_A, n, HEAD_LANES), lambda b: (0, b, 0))
    old = pl.BlockSpec((None, None, cache_rows, HEAD_LANES), lambda b: (layer, b, 0, 0))
    return pl.pallas_call(
        functools.partial(_attn_a_sample_kernel, lam_scale=lam_scale),
        grid=(nb,),
        in_specs=[pl.BlockSpec(memory_space=pltpu.SMEM), rows, old, old, new, new,
                  pl.BlockSpec((1, DV_A), lambda b: (0, 0))],
        out_specs=rows,
        out_shape=jax.ShapeDtypeStruct(qa.shape, BF16),
        compiler_params=_params("parallel"),
        name="attn_a_sample",
    )(lam, qa, cache_k, cache_v, k16, v16, subln)


def _toeplitz_bias(ruler_row, nq, nk):
    rows = jnp.broadcast_to(ruler_row, (nq, ruler_row.shape[1]))
    return pltpu.roll(rows, 0, 1, stride=1, stride_axis=0)[:, :nk]


def _attn_b_prompt_kernel(q_ref, k_ref, v_ref, ruler_ref, o_ref, kb_ref, vb_ref, bias_ref):
    qi = pl.program_id(2)
    bq = q_ref.shape[0]
    ng, seq, _ = k_ref.shape

    @pl.when(qi == 0)
    def _():
        zeros = jnp.zeros((BAND_ROWS, HEAD_LANES), BF16)
        d = (lax.broadcasted_iota(jnp.int32, (bq, BAND_WIN), 1) // CHUNK
             - lax.broadcasted_iota(jnp.int32, (bq, BAND_WIN), 0) // CHUNK)
        for g in range(ng):
            kb_ref[g, 0:BAND_ROWS, :] = zeros
            vb_ref[g, 0:BAND_ROWS, :] = zeros
            kb_ref[g, BAND_ROWS:BAND_ROWS + seq, :] = k_ref[g]
            vb_ref[g, BAND_ROWS:BAND_ROWS + seq, :] = v_ref[g]
            bias = _toeplitz_bias(ruler_ref[g], bq, BAND_WIN)
            bias_ref[g] = jnp.where(d >= 0, jnp.where(d <= N_PREV, bias, NEG), NEG)

    start = pl.multiple_of(qi * bq, bq)
    win = pl.ds(start, BAND_WIN)
    scores = [_qk(q_ref[:, g * HEAD_LANES:(g + 1) * HEAD_LANES], kb_ref[g, win, :]) for g in range(ng)]
    w = lax.broadcasted_iota(jnp.int32, (bq, BAND_WIN), 1)
    in_sequence = w + qi * bq >= BAND_ROWS
    for g in range(ng):
        s = scores[g] * (DH_B ** -0.5) + bias_ref[g]
        s = jnp.where(in_sequence, s, NEG)
        p = jnp.exp(s - jnp.max(s, axis=-1, keepdims=True))
        l = jnp.sum(p, axis=-1, keepdims=True)
        o = jnp.dot(p.astype(BF16), vb_ref[g, win, :], preferred_element_type=F32) / l
        o_ref[:, g * HEAD_LANES:(g + 1) * HEAD_LANES] = o.astype(o_ref.dtype)


def _attn_b_prompt(qb, k16, v16, ruler, batch, seq):
    bq, ng = ATT_BQ, BAND_HEADS_PER_STEP
    nq = seq // bq
    kv = pl.BlockSpec((ng, seq, HEAD_LANES), lambda b, h, i: (h, b, 0))
    rows = pl.BlockSpec((bq, ng * HEAD_LANES), lambda b, h, i: (b * nq + i, h))
    return pl.pallas_call(
        _attn_b_prompt_kernel,
        grid=(batch, H_B // ng, nq),
        in_specs=[rows, kv, kv, pl.BlockSpec((ng, 1, BIAS_RULER), lambda b, h, i: (h, 0, 0))],
        out_specs=rows,
        out_shape=jax.ShapeDtypeStruct((batch * seq, W_B), BF16),
        scratch_shapes=[pltpu.VMEM((ng, BAND_ROWS + seq, HEAD_LANES), BF16),
                        pltpu.VMEM((ng, BAND_ROWS + seq, HEAD_LANES), BF16),
                        pltpu.VMEM((ng, bq, BAND_WIN), F32)],
        compiler_params=_params("parallel", "parallel", "arbitrary"),
        name="attn_b_prompt",
    )(qb, k16, v16, ruler)


def _attn_b_sample_kernel(q_ref, kc_ref, vc_ref, kn_ref, vn_ref, ruler_ref, o_ref, bias_ref):
    n = q_ref.shape[0]
    past = kc_ref.shape[0] // H_B
    scale = DH_B ** -0.5

    @pl.when(pl.program_id(0) == 0)
    def _():
        for h in range(H_B):
            bias_ref[h] = _toeplitz_bias(ruler_ref[h], n, past + n)

    for h in range(H_B):
        cols = slice(h * HEAD_LANES, (h + 1) * HEAD_LANES)
        head_rows = pl.ds(h, past, stride=H_B)
        q = q_ref[:, cols]
        sc = _qk(q, kc_ref[head_rows, :].astype(BF16)) * scale + bias_ref[h, :, :past]
        sn = _qk(q, kn_ref[h]) * scale + bias_ref[h, :, past:]
        acc, l = _softmax_two_parts(sc, sn, vc_ref[head_rows, :].astype(BF16), vn_ref[h])
        o_ref[:, cols] = (acc / l).astype(o_ref.dtype)


def _attn_b_sample(qb, cache_k, cache_v, k16, v16, ruler, layer):
    _, nb, cache_rows, _ = cache_k.shape
    past = cache_rows // H_B
    n = qb.shape[0] // nb
    rows = pl.BlockSpec((n, W_B), lambda b: (b, 0))
    new = pl.BlockSpec((H_B, n, HEAD_LANES), lambda b: (0, b, 0))
    old = pl.BlockSpec((None, None, cache_rows, HEAD_LANES), lambda b: (layer, b, 0, 0))
    return pl.pallas_call(
        _attn_b_sample_kernel,
        grid=(nb,),
        in_specs=[rows, old, old, new, new,
                  pl.BlockSpec((H_B, 1, BIAS_RULER), lambda b: (0, 0, 0))],
        out_specs=rows,
        out_shape=jax.ShapeDtypeStruct(qb.shape, BF16),
        scratch_shapes=[pltpu.VMEM((H_B, n, past + n), F32)],
        compiler_params=_params("arbitrary"),
        name="attn_b_sample",
    )(qb, cache_k, cache_v, k16, v16, ruler)


def _merge_kernel(x_ref, oa_ref, ob_ref, sga_ref, sgb_ref, wa_ref, wb_ref, wo_ref, o_ref):
    a = jnp.dot(oa_ref[...], wa_ref[...], preferred_element_type=F32)
    b = jnp.dot(ob_ref[...], wb_ref[...], preferred_element_type=F32)
    merged = sga_ref[...].astype(F32) * a + sgb_ref[...].astype(F32) * b
    o_ref[...] = x_ref[...] + jnp.dot(merged.astype(BF16), wo_ref[...], preferred_element_type=F32)


def _merge(x, oa, ob, sga, sgb, wa, wb, wo):
    m = x.shape[0]
    bm = MERGE_BM
    wide = pl.BlockSpec((bm, D_MODEL), lambda i: (i, 0))
    half = pl.BlockSpec((bm, W_A), lambda i: (i, 0))

    def whole(shape):
        return pl.BlockSpec(shape, lambda i: (0, 0))

    return pl.pallas_call(
        _merge_kernel,
        grid=(m // bm,),
        in_specs=[wide, half, half, wide, wide,
                  whole((W_A, D_MODEL)), whole((W_B, D_MODEL)), whole((D_MODEL, D_MODEL))],
        out_specs=wide,
        out_shape=jax.ShapeDtypeStruct((m, D_MODEL), F32),
        compiler_params=_params("parallel"),
        name="merge",
    )(x, oa, ob, sga, sgb, wa, wb, wo)


def _rope_tables(pos):
    half = DK_A // 2
    inv = ROPE_THETA ** (-jnp.arange(half, dtype=F32) / half)
    ang = pos.astype(F32)[:, None] * inv[None, :]
    cos, sin = jnp.cos(ang), jnp.sin(ang)
    reps = HEAD_LANES // DK_A
    cos_t = jnp.tile(jnp.concatenate([cos, cos], axis=1), (1, reps))
    sin_t = jnp.tile(jnp.concatenate([-sin, sin], axis=1), (1, reps))
    return cos_t, sin_t


def _bias_ruler(table):
    nh = table.shape[0]
    far_past = jnp.broadcast_to(table[:, 2 * MAX_REL:], (nh, BAND_ROWS - MAX_REL))
    far_future = jnp.broadcast_to(table[:, :1], (nh, BIAS_RULER - ATT_BQ - BAND_ROWS - MAX_REL - 1))
    wrapped = jnp.broadcast_to(table[:, 2 * MAX_REL:], (nh, ATT_BQ))
    ruler = jnp.concatenate([far_past, table[:, ::-1], far_future, wrapped], axis=1)
    return ruler.astype(F32).reshape(nh, 1, BIAS_RULER)


def kernel(x_prompt, x_sample, cache_a_k, cache_a_v, cache_b_k, cache_b_v, ffn1_norm, ffn1_w_gate, ffn1_w_up, ffn1_w_down, mix_norm, w_in, lambda_q1, lambda_k1, lambda_q2, lambda_k2, subln_a, rel_bias_b, w_branch_a, w_branch_b, w_out, ffn2_norm, ffn2_w_gate, ffn2_w_up, ffn2_w_down, final_norm):
    batch, seq, _ = x_prompt.shape
    dec_batch, dec_seq, _ = x_sample.shape
    depth = ffn1_norm.shape[0]
    past_a = cache_a_k.shape[2]
    past_b = cache_b_k.shape[2]
    assert seq % max(ATT_BQ, PROJ_BM) == 0 and seq >= BAND_ROWS
    assert dec_seq == CHUNK and past_b == BAND_ROWS and PROJ_BM % dec_seq == 0

    xp = x_prompt.reshape(batch * seq, D_MODEL)
    xs = x_sample.reshape(dec_batch * dec_seq, D_MODEL)
    rope_p = _rope_tables(jnp.arange(seq))
    rope_s = _rope_tables(past_a + (jnp.arange(PROJ_BM) % dec_seq))
    cache_a_k2 = cache_a_k.reshape(depth, dec_batch, past_a * H_A, HEAD_LANES)
    cache_a_v2 = cache_a_v.reshape(depth, dec_batch, past_a * H_A, HEAD_LANES)
    cache_b_k2 = cache_b_k.reshape(depth, dec_batch, past_b * H_B, HEAD_LANES)
    cache_b_v2 = cache_b_v.reshape(depth, dec_batch, past_b * H_B, HEAD_LANES)

    outs = {k: [] for k in ("akp", "avp", "bkp", "bvp", "aks", "avs", "bks", "bvs")}
    for layer in range(depth):
        lam_init = 0.8 - 0.6 * math.exp(-0.3 * layer)
        lam = (jnp.exp(jnp.sum(lambda_q1[layer].astype(F32) * lambda_k1[layer].astype(F32)))
               - jnp.exp(jnp.sum(lambda_q2[layer].astype(F32) * lambda_k2[layer].astype(F32)))
               + lam_init).reshape(1)
        lam_scale = 1.0 - lam_init
        g1 = ffn1_norm[layer].reshape(1, D_MODEL)
        gm = mix_norm[layer].reshape(1, D_MODEL)
        g2 = ffn2_norm[layer].reshape(1, D_MODEL)
        gf = final_norm.reshape(1, D_MODEL)
        subln = subln_a[layer].reshape(1, DV_A)
        w1g, w1u, w1d = (w[layer].astype(BF16) for w in (ffn1_w_gate, ffn1_w_up, ffn1_w_down))
        table = rel_bias_b[layer]
        ruler = _bias_ruler(table)
        last = layer == depth - 1

        later = [w[layer] for w in (ffn2_w_gate, ffn2_w_up, ffn2_w_down, w_in, w_branch_a, w_branch_b, w_out)]
        x1_p, hn_p, w2g, w2u, w2d, win, wa, wb, wo = _ffn(xp, g1, w1g, w1u, w1d, gm, "mix", casts=later)
        x1_s, hn_s = _ffn(xs, g1, w1g, w1u, w1d, gm, "mix")

        def trunk(x1, hn, rope_tabs, rope_period, attend, transposed_a):
            c = 0
            qa = _proj(hn, win, c, W_QK_A, "rope_q", rope_tabs, rope_period, transposed=transposed_a); c += W_QK_A
            ka, ka16 = _proj(hn, win, c, W_QK_A, "rope_kv", rope_tabs, rope_period); c += W_QK_A
            va, va16 = _proj(hn, win, c, W_A, "kv", transposed=transposed_a); c += W_A
            qb = _proj(hn, win, c, W_B, "plain"); c += W_B
            kb, kb16 = _proj(hn, win, c, W_B, "kv"); c += W_B
            vb, vb16 = _proj(hn, win, c, W_B, "kv"); c += W_B
            sga = _proj(hn, win, c, D_MODEL, "sigmoid"); c += D_MODEL
            sgb = _proj(hn, win, c, D_MODEL, "sigmoid")
            oa, ob = attend(qa, ka16, va16, qb, kb16, vb16)
            x2 = _merge(x1, oa, ob, sga, sgb, wa, wb, wo)
            xo = _ffn(x2, g2, w2g, w2u, w2d, gf, "final" if last else "mix")[0]
            return xo, ka, va, kb, vb

        def attend_prompt(qa, ka16, va16, qb, kb16, vb16):
            oa = _attn_a_prompt(lam, qa, ka16, va16, subln, batch, seq, lam_scale)
            ob = _attn_b_prompt(qb, kb16, vb16, ruler, batch, seq)
            return oa, ob

        def attend_sample(qa, ka16, va16, qb, kb16, vb16):
            oa = _attn_a_sample(lam, qa, cache_a_k2, cache_a_v2, ka16, va16, subln, layer, lam_scale)
            ob = _attn_b_sample(qb, cache_b_k2, cache_b_v2, kb16, vb16, ruler, layer)
            return oa, ob

        xp, ka, va, kb, vb = trunk(x1_p, hn_p, rope_p, seq, attend_prompt, True)
        outs["akp"].append(ka.reshape(batch, seq, H_A, 2 * DK_A))
        outs["avp"].append(va.reshape(batch, seq, H_A, DV_A))
        rows = min(BAND_ROWS, seq)
        outs["bkp"].append(kb.reshape(batch, seq, H_B, DH_B)[:, seq - rows:])
        outs["bvp"].append(vb.reshape(batch, seq, H_B, DH_B)[:, seq - rows:])

        xs, ka, va, kb, vb = trunk(x1_s, hn_s, rope_s, PROJ_BM, attend_sample, False)
        outs["aks"].append(ka.reshape(dec_batch, dec_seq, H_A, 2 * DK_A))
        outs["avs"].append(va.reshape(dec_batch, dec_seq, H_A, DV_A))
        kb_all = jnp.concatenate([cache_b_k[layer], kb.reshape(dec_batch, dec_seq, H_B, DH_B)], axis=1)
        vb_all = jnp.concatenate([cache_b_v[layer], vb.reshape(dec_batch, dec_seq, H_B, DH_B)], axis=1)
        outs["bks"].append(kb_all[:, dec_seq:])
        outs["bvs"].append(vb_all[:, dec_seq:])

    y_prompt = xp.reshape(batch, seq, D_MODEL)
    y_sample = xs.reshape(dec_batch, dec_seq, D_MODEL)
    return (y_prompt, y_sample,
            jnp.stack(outs["akp"]), jnp.stack(outs["avp"]), jnp.stack(outs["bkp"]), jnp.stack(outs["bvp"]),
            jnp.stack(outs["aks"]), jnp.stack(outs["avs"]), jnp.stack(outs["bks"]), jnp.stack(outs["bvs"]))
```

```python
import functools
import math

import jax
import jax.numpy as jnp
from jax import lax
from jax.experimental import pallas as pl
from jax.experimental.pallas import tpu as pltpu

F32 = jnp.float32
BF16 = jnp.bfloat16

D_MODEL = 2048
CHUNK = 64
H_A = 8
DK_A = 64
DV_A = 2 * DK_A
W_QK_A = H_A * 2 * DK_A
W_A = H_A * DV_A
H_B = 8
DH_B = 128
W_B = H_B * DH_B
N_PREV = 8
BAND_ROWS = N_PREV * CHUNK
MAX_REL = 128
FFN_DIM = 4 * D_MODEL
ROPE_THETA = 10000.0
EPS = 1e-6
NEG = -1e30

HEAD_LANES = 128
BF16_SUBLANES = 16
VMEM_LIMIT = 58 * 1024 * 1024

FFN_BM = 512
FFN_BF = 1024
PROJ_BM = 1024
PROJ_BN = 1024
MERGE_BM = 512
ATT_BQ = 256
BAND_HEADS_PER_STEP = 4
BAND_WIN = BAND_ROWS + ATT_BQ
BIAS_RULER = BAND_WIN + ATT_BQ


def _params(*sem):
    return pltpu.CompilerParams(dimension_semantics=sem, vmem_limit_bytes=VMEM_LIMIT)


def _rms(x, g):
    return x * lax.rsqrt(jnp.mean(x * x, axis=-1, keepdims=True) + EPS) * g


def _ffn_kernel(x_ref, g_ref, wg_ref, wu_ref, wd_ref, g2_ref, *rest, mode, ncast):
    cast_in, rest = rest[:ncast], rest[ncast:]
    if mode == "mix":
        o_ref, hn_ref, *rest = rest
    else:
        o_ref, *rest = rest
    cast_out, (h_ref,) = rest[:ncast], rest[ncast:]
    f = pl.program_id(1)

    for src, dst in zip(cast_in, cast_out):
        dst[...] = src[...].astype(BF16)

    @pl.when(f == 0)
    def _():
        x = x_ref[...]
        h_ref[...] = _rms(x, g_ref[...]).astype(BF16)
        o_ref[...] = x

    h = h_ref[...]
    a = jnp.dot(h, wg_ref[...], preferred_element_type=F32)
    b = jnp.dot(h, wu_ref[...], preferred_element_type=F32)
    t = (0.5 * (a * jax.nn.sigmoid(a)) * b).astype(BF16)
    o_ref[...] += jnp.dot(t, wd_ref[...], preferred_element_type=F32)

    @pl.when(f == pl.num_programs(1) - 1)
    def _():
        y = _rms(o_ref[...], g2_ref[...])
        if mode == "mix":
            hn_ref[...] = y.astype(BF16)
        else:
            o_ref[...] = y


def _ffn(x, g, wg, wu, wd, g2, mode, casts=()):
    m = x.shape[0]
    bm, bf = FFN_BM, FFN_BF
    grid = (m // bm, FFN_DIM // bf)
    nf = grid[1]
    row = pl.BlockSpec((bm, D_MODEL), lambda i, f: (i, 0))
    vec = pl.BlockSpec((1, D_MODEL), lambda i, f: (0, 0))
    in_specs = [
        row, vec,
        pl.BlockSpec((D_MODEL, bf), lambda i, f: (0, f)),
        pl.BlockSpec((D_MODEL, bf), lambda i, f: (0, f)),
        pl.BlockSpec((bf, D_MODEL), lambda i, f: (f, 0)),
        vec,
    ]
    if mode == "mix":
        out_shape = [jax.ShapeDtypeStruct((m, D_MODEL), F32), jax.ShapeDtypeStruct((m, D_MODEL), BF16)]
        out_specs = [row, row]
    else:
        out_shape = [jax.ShapeDtypeStruct((m, D_MODEL), F32)]
        out_specs = [row]
    nsteps = grid[0] * nf
    for w in casts:
        rows, cols = w.shape
        if rows % (nsteps * BF16_SUBLANES) == 0:
            block_rows, hold = rows // nsteps, 1
        else:
            block_rows, hold = BF16_SUBLANES, nsteps * BF16_SUBLANES // rows
            assert rows * hold == nsteps * BF16_SUBLANES
        spec = pl.BlockSpec((block_rows, cols), functools.partial(lambda i, f, hold: ((i * nf + f) // hold, 0), hold=hold))
        in_specs.append(spec)
        out_specs.append(spec)
        out_shape.append(jax.ShapeDtypeStruct(w.shape, BF16))
    return pl.pallas_call(
        functools.partial(_ffn_kernel, mode=mode, ncast=len(casts)),
        grid=grid, in_specs=in_specs, out_specs=out_specs, out_shape=out_shape,
        scratch_shapes=[pltpu.VMEM((bm, D_MODEL), BF16)],
        compiler_params=_params("parallel", "arbitrary"),
        name="ffn_" + mode,
    )(x, g, wg, wu, wd, g2, *casts)


def _rope_slabs(z, cos, sin_signed):
    lane = lax.broadcasted_iota(jnp.int32, cos.shape, 1)
    first_half = (lane % DK_A) < (DK_A // 2)
    outs = []
    for s in range(z.shape[1] // HEAD_LANES):
        slab = z[:, s * HEAD_LANES:(s + 1) * HEAD_LANES]
        partner = jnp.where(first_half,
                            pltpu.roll(slab, HEAD_LANES - DK_A // 2, 1),
                            pltpu.roll(slab, DK_A // 2, 1))
        outs.append(slab * cos + partner * sin_signed)
    return jnp.concatenate(outs, axis=1)


def _proj_kernel(h_ref, w_ref, *rest, epilogue, transposed):
    rope = epilogue in ("rope_q", "rope_kv")
    heads = epilogue in ("rope_kv", "kv")
    if rope:
        cos_ref, sin_ref, *rest = rest
    z = jnp.dot(h_ref[...], w_ref[...], preferred_element_type=F32)
    if rope:
        z = _rope_slabs(z, cos_ref[...], sin_ref[...])
    if epilogue == "rope_q":
        z = z * (DK_A ** -0.5)
    elif epilogue == "sigmoid":
        z = jax.nn.sigmoid(z)
    nh = z.shape[1] // HEAD_LANES
    if heads:
        o_ref, o16_ref = rest
        for h in range(nh):
            zh = z[:, h * HEAD_LANES:(h + 1) * HEAD_LANES]
            o_ref[pl.ds(h, z.shape[0], stride=nh), :] = zh
            o16_ref[h] = (zh.T if transposed else zh).astype(BF16)
    elif transposed:
        (o_ref,) = rest
        for h in range(nh):
            o_ref[h] = z[:, h * HEAD_LANES:(h + 1) * HEAD_LANES].T.astype(BF16)
    else:
        (o_ref,) = rest
        o_ref[...] = z.astype(o_ref.dtype)


def _proj(hn, w_in, col0, ncols, epilogue, rope_tabs=None, rope_period=None, transposed=False):
    m = hn.shape[0]
    bm, bn = PROJ_BM, PROJ_BN
    joff = col0 // bn
    nh = bn // HEAD_LANES
    head_major = (pl.BlockSpec((nh, HEAD_LANES, bm), lambda i, j: (0, 0, i)) if transposed
                  else pl.BlockSpec((nh, bm, HEAD_LANES), lambda i, j: (0, i, 0)))
    head_major_shape = jax.ShapeDtypeStruct((nh, HEAD_LANES, m) if transposed else (nh, m, HEAD_LANES), BF16)
    in_specs = [
        pl.BlockSpec((bm, D_MODEL), lambda i, j: (i, 0)),
        pl.BlockSpec((D_MODEL, bn), lambda i, j: (0, j + joff)),
    ]
    args = [hn, w_in]
    if rope_tabs is not None:
        nper = rope_period // bm
        tab = pl.BlockSpec((bm, HEAD_LANES), lambda i, j: (i % nper, 0))
        in_specs += [tab, tab]
        args += list(rope_tabs)
    if epilogue in ("rope_kv", "kv"):
        assert ncols == bn
        out_specs = (pl.BlockSpec((bm * nh, HEAD_LANES), lambda i, j: (i, 0)), head_major)
        out_shape = (jax.ShapeDtypeStruct((m * nh, HEAD_LANES), F32), head_major_shape)
    elif transposed:
        assert ncols == bn
        out_specs, out_shape = head_major, head_major_shape
    else:
        out_specs = pl.BlockSpec((bm, bn), lambda i, j: (i, j))
        out_shape = jax.ShapeDtypeStruct((m, ncols), BF16)
    return pl.pallas_call(
        functools.partial(_proj_kernel, epilogue=epilogue, transposed=transposed),
        grid=(m // bm, ncols // bn), in_specs=in_specs,
        out_specs=out_specs, out_shape=out_shape,
        compiler_params=_params("parallel", "arbitrary"),
        name="proj_" + epilogue,
    )(*args)


def _block_diag_q(q):
    lane = lax.broadcasted_iota(jnp.int32, q.shape, 1)
    zero = jnp.zeros_like(q)
    return jnp.concatenate([jnp.where(lane < DK_A, q, zero), jnp.where(lane >= DK_A, q, zero)], axis=0)


def _qk(q, k):
    return lax.dot_general(q, k, (((1,), (1,)), ((), ())), preferred_element_type=F32)


def _attn_a_prompt_kernel(lam_ref, qt_ref, k_ref, vt_ref, subln_ref, o_ref, *, lam_scale):
    qi = pl.program_id(1)
    nh, _, bq = qt_ref.shape
    row = lax.broadcasted_iota(jnp.int32, (HEAD_LANES, bq), 0)
    qbd = []
    for h in range(nh):
        qt = qt_ref[h]
        zero = jnp.zeros_like(qt)
        qbd.append(jnp.concatenate([jnp.where(row < DK_A, qt, zero), jnp.where(row >= DK_A, qt, zero)], axis=1))

    def step(j, carries, masked):
        start = pl.multiple_of(j * bq, bq)
        scores = [jnp.dot(k_ref[h, pl.ds(start, bq), :], qbd[h], preferred_element_type=F32) for h in range(nh)]
        if masked:
            kr = lax.broadcasted_iota(jnp.int32, (bq, 2 * bq), 0)
            qc = lax.broadcasted_iota(jnp.int32, (bq, 2 * bq), 1) % bq
            visible = (kr // CHUNK) <= (qc // CHUNK)
        out = []
        for h in range(nh):
            m, l, acc = carries[h]
            s = scores[h]
            if masked:
                s = jnp.where(visible, s, NEG)
            m_new = jnp.maximum(m, jnp.max(s, axis=0, keepdims=True))
            alpha = jnp.exp(m - m_new)
            p = jnp.exp(s - m_new)
            l = alpha * l + jnp.sum(p, axis=0, keepdims=True)
            acc = alpha * acc + jnp.dot(vt_ref[h, :, pl.ds(start, bq)], p.astype(BF16),
                                        preferred_element_type=F32)
            out.append((m_new, l, acc))
        return tuple(out)

    init = (jnp.full((1, 2 * bq), -jnp.inf, F32), jnp.zeros((1, 2 * bq), F32),
            jnp.zeros((HEAD_LANES, 2 * bq), F32))
    carries = lax.fori_loop(0, qi, functools.partial(step, masked=False), (init,) * nh)
    carries = step(qi, carries, masked=True)
    for h in range(nh):
        _, l, acc = carries[h]
        o = acc / l
        o = (o[:, :bq] - lam_ref[0] * o[:, bq:]).T
        o = _rms(o, subln_ref[...]) * lam_scale
        o_ref[:, h * HEAD_LANES:(h + 1) * HEAD_LANES] = o.astype(o_ref.dtype)


def _attn_a_prompt(lam, qt, k16, vt, subln, batch, seq, lam_scale):
    bq = ATT_BQ
    nq = seq // bq
    return pl.pallas_call(
        functools.partial(_attn_a_prompt_kernel, lam_scale=lam_scale),
        grid=(batch, nq),
        in_specs=[
            pl.BlockSpec(memory_space=pltpu.SMEM),
            pl.BlockSpec((H_A, HEAD_LANES, bq), lambda b, i: (0, 0, b * nq + i)),
            pl.BlockSpec((H_A, seq, HEAD_LANES), lambda b, i: (0, b, 0)),
            pl.BlockSpec((H_A, HEAD_LANES, seq), lambda b, i: (0, 0, b)),
            pl.BlockSpec((1, DV_A), lambda b, i: (0, 0)),
        ],
        out_specs=pl.BlockSpec((bq, W_A), lambda b, i: (b * nq + i, 0)),
        out_shape=jax.ShapeDtypeStruct((batch * seq, W_A), BF16),
        compiler_params=_params("parallel", "arbitrary"),
        name="attn_a_prompt",
    )(lam, qt, k16, vt, subln)


def _softmax_two_parts(sc, sn, vc, vn):
    m = jnp.maximum(jnp.max(sc, axis=-1, keepdims=True), jnp.max(sn, axis=-1, keepdims=True))
    pc = jnp.exp(sc - m)
    pn = jnp.exp(sn - m)
    l = jnp.sum(pc, axis=-1, keepdims=True) + jnp.sum(pn, axis=-1, keepdims=True)
    acc = (jnp.dot(pc.astype(BF16), vc, preferred_element_type=F32)
           + jnp.dot(pn.astype(BF16), vn, preferred_element_type=F32))
    return acc, l


def _tn(a, b):
    return lax.dot_general(a, b, (((0,), (0,)), ((), ())), preferred_element_type=F32)


def _attn_a_sample_kernel(lam_ref, q_ref, kc_ref, vc_ref, kn_ref, vn_ref, subln_ref, o_ref, *, lam_scale):
    n = q_ref.shape[0]
    past = kc_ref.shape[0] // H_A
    scores = []
    for h in range(H_A):
        qbd_t = _block_diag_q(q_ref[:, h * HEAD_LANES:(h + 1) * HEAD_LANES]).astype(F32).T.astype(BF16)
        kc = kc_ref[pl.ds(h, past, stride=H_A), :].astype(BF16)
        scores.append((jnp.dot(kc, qbd_t, preferred_element_type=F32),
                       jnp.dot(kn_ref[h], qbd_t, preferred_element_type=F32)))
    for h in range(H_A):
        sc, sn = scores[h]
        m = jnp.maximum(jnp.max(sc, axis=0, keepdims=True), jnp.max(sn, axis=0, keepdims=True))
        pc = jnp.exp(sc - m)
        pn = jnp.exp(sn - m)
        l = jnp.sum(pc, axis=0, keepdims=True) + jnp.sum(pn, axis=0, keepdims=True)
        acc = (_tn(vc_ref[pl.ds(h, past, stride=H_A), :].astype(BF16), pc.astype(BF16))
               + _tn(vn_ref[h], pn.astype(BF16)))
        o = acc / l
        o = (o[:, :n] - lam_ref[0] * o[:, n:]).T
        o = _rms(o, subln_ref[...]) * lam_scale
        o_ref[:, h * HEAD_LANES:(h + 1) * HEAD_LANES] = o.astype(o_ref.dtype)


def _attn_a_sample(lam, qa, cache_k, cache_v, k16, v16, subln, layer, lam_scale):
    _, nb, cache_rows, _ = cache_k.shape
    n = qa.shape[0] // nb
    rows = pl.BlockSpec((n, W_A), lambda b: (b, 0))
    new = pl.BlockSpec((H_A, n, HEAD_LANES), lambda b: (0, b, 0))
    old = pl.BlockSpec((None, None, cache_rows, HEAD_LANES), lambda b: (layer, b, 0, 0))
    return pl.pallas_call(
        functools.partial(_attn_a_sample_kernel, lam_scale=lam_scale),
        grid=(nb,),
        in_specs=[pl.BlockSpec(memory_space=pltpu.SMEM), rows, old, old, new, new,
                  pl.BlockSpec((1, DV_A), lambda b: (0, 0))],
        out_specs=rows,
        out_shape=jax.ShapeDtypeStruct(qa.shape, BF16),
        compiler_params=_params("parallel"),
        name="attn_a_sample",
    )(lam, qa, cache_k, cache_v, k16, v16, subln)


def _toeplitz_bias(ruler_row, nq, nk):
    rows = jnp.broadcast_to(ruler_row, (nq, ruler_row.shape[1]))
    return pltpu.roll(rows, 0, 1, stride=1, stride_axis=0)[:, :nk]


def _attn_b_prompt_kernel(q_ref, k_ref, v_ref, ruler_ref, o_ref, kb_ref, vb_ref, bias_ref):
    qi = pl.program_id(2)
    bq = q_ref.shape[0]
    ng, seq, _ = k_ref.shape

    @pl.when(qi == 0)
    def _():
        zeros = jnp.zeros((BAND_ROWS, HEAD_LANES), BF16)
        d = (lax.broadcasted_iota(jnp.int32, (bq, BAND_WIN), 1) // CHUNK
             - lax.broadcasted_iota(jnp.int32, (bq, BAND_WIN), 0) // CHUNK)
        for g in range(ng):
            kb_ref[g, 0:BAND_ROWS, :] = zeros
            vb_ref[g, 0:BAND_ROWS, :] = zeros
            kb_ref[g, BAND_ROWS:BAND_ROWS + seq, :] = k_ref[g]
            vb_ref[g, BAND_ROWS:BAND_ROWS + seq, :] = v_ref[g]
            bias = _toeplitz_bias(ruler_ref[g], bq, BAND_WIN)
            bias_ref[g] = jnp.where(d >= 0, jnp.where(d <= N_PREV, bias, NEG), NEG)

    start = pl.multiple_of(qi * bq, bq)
    win = pl.ds(start, BAND_WIN)
    scores = [_qk(q_ref[:, g * HEAD_LANES:(g + 1) * HEAD_LANES], kb_ref[g, win, :]) for g in range(ng)]
    w = lax.broadcasted_iota(jnp.int32, (bq, BAND_WIN), 1)
    in_sequence = w + qi * bq >= BAND_ROWS
    for g in range(ng):
        s = scores[g] * (DH_B ** -0.5) + bias_ref[g]
        s = jnp.where(in_sequence, s, NEG)
        p = jnp.exp(s - jnp.max(s, axis=-1, keepdims=True))
        l = jnp.sum(p, axis=-1, keepdims=True)
        o = jnp.dot(p.astype(BF16), vb_ref[g, win, :], preferred_element_type=F32) / l
        o_ref[:, g * HEAD_LANES:(g + 1) * HEAD_LANES] = o.astype(o_ref.dtype)


def _attn_b_prompt(qb, k16, v16, ruler, batch, seq):
    bq, ng = ATT_BQ, BAND_HEADS_PER_STEP
    nq = seq // bq
    kv = pl.BlockSpec((ng, seq, HEAD_LANES), lambda b, h, i: (h, b, 0))
    rows = pl.BlockSpec((bq, ng * HEAD_LANES), lambda b, h, i: (b * nq + i, h))
    return pl.pallas_call(
        _attn_b_prompt_kernel,
        grid=(batch, H_B // ng, nq),
        in_specs=[rows, kv, kv, pl.BlockSpec((ng, 1, BIAS_RULER), lambda b, h, i: (h, 0, 0))],
        out_specs=rows,
        out_shape=jax.ShapeDtypeStruct((batch * seq, W_B), BF16),
        scratch_shapes=[pltpu.VMEM((ng, BAND_ROWS + seq, HEAD_LANES), BF16),
                        pltpu.VMEM((ng, BAND_ROWS + seq, HEAD_LANES), BF16),
                        pltpu.VMEM((ng, bq, BAND_WIN), F32)],
        compiler_params=_params("parallel", "parallel", "arbitrary"),
        name="attn_b_prompt",
    )(qb, k16, v16, ruler)


def _attn_b_sample_kernel(q_ref, kc_ref, vc_ref, kn_ref, vn_ref, ruler_ref, o_ref, bias_ref):
    n = q_ref.shape[0]
    past = kc_ref.shape[0] // H_B
    scale = DH_B ** -0.5

    @pl.when(pl.program_id(0) == 0)
    def _():
        for h in range(H_B):
            bias_ref[h] = _toeplitz_bias(ruler_ref[h], n, past + n)

    scores = []
    for h in range(H_B):
        q = q_ref[:, h * HEAD_LANES:(h + 1) * HEAD_LANES]
        scores.append((_qk(q, kc_ref[pl.ds(h, past, stride=H_B), :].astype(BF16)), _qk(q, kn_ref[h])))
    for h in range(H_B):
        sc = scores[h][0] * scale + bias_ref[h, :, :past]
        sn = scores[h][1] * scale + bias_ref[h, :, past:]
        acc, l = _softmax_two_parts(sc, sn, vc_ref[pl.ds(h, past, stride=H_B), :].astype(BF16), vn_ref[h])
        o_ref[:, h * HEAD_LANES:(h + 1) * HEAD_LANES] = (acc / l).astype(o_ref.dtype)


def _attn_b_sample(qb, cache_k, cache_v, k16, v16, ruler, layer):
    _, nb, cache_rows, _ = cache_k.shape
    past = cache_rows // H_B
    n = qb.shape[0] // nb
    rows = pl.BlockSpec((n, W_B), lambda b: (b, 0))
    new = pl.BlockSpec((H_B, n, HEAD_LANES), lambda b: (0, b, 0))
    old = pl.BlockSpec((None, None, cache_rows, HEAD_LANES), lambda b: (layer, b, 0, 0))
    return pl.pallas_call(
        _attn_b_sample_kernel,
        grid=(nb,),
        in_specs=[rows, old, old, new, new,
                  pl.BlockSpec((H_B, 1, BIAS_RULER), lambda b: (0, 0, 0))],
        out_specs=rows,
        out_shape=jax.ShapeDtypeStruct(qb.shape, BF16),
        scratch_shapes=[pltpu.VMEM((H_B, n, past + n), F32)],
        compiler_params=_params("arbitrary"),
        name="attn_b_sample",
    )(qb, cache_k, cache_v, k16, v16, ruler)


def _merge_kernel(x_ref, oa_ref, ob_ref, sga_ref, sgb_ref, wa_ref, wb_ref, wo_ref, o_ref):
    a = jnp.dot(oa_ref[...], wa_ref[...], preferred_element_type=F32)
    b = jnp.dot(ob_ref[...], wb_ref[...], preferred_element_type=F32)
    merged = sga_ref[...].astype(F32) * a + sgb_ref[...].astype(F32) * b
    o_ref[...] = x_ref[...] + jnp.dot(merged.astype(BF16), wo_ref[...], preferred_element_type=F32)


def _merge(x, oa, ob, sga, sgb, wa, wb, wo):
    m = x.shape[0]
    bm = MERGE_BM
    wide = pl.BlockSpec((bm, D_MODEL), lambda i: (i, 0))
    half = pl.BlockSpec((bm, W_A), lambda i: (i, 0))

    def whole(shape):
        return pl.BlockSpec(shape, lambda i: (0, 0), pipeline_mode=pl.Buffered(1))

    return pl.pallas_call(
        _merge_kernel,
        grid=(m // bm,),
        in_specs=[wide, half, half, wide, wide,
                  whole((W_A, D_MODEL)), whole((W_B, D_MODEL)), whole((D_MODEL, D_MODEL))],
        out_specs=wide,
        out_shape=jax.ShapeDtypeStruct((m, D_MODEL), F32),
        compiler_params=_params("parallel"),
        name="merge",
    )(x, oa, ob, sga, sgb, wa, wb, wo)


def _rope_tables(pos):
    half = DK_A // 2
    inv = ROPE_THETA ** (-jnp.arange(half, dtype=F32) / half)
    ang = pos.astype(F32)[:, None] * inv[None, :]
    cos, sin = jnp.cos(ang), jnp.sin(ang)
    reps = HEAD_LANES // DK_A
    cos_t = jnp.tile(jnp.concatenate([cos, cos], axis=1), (1, reps))
    sin_t = jnp.tile(jnp.concatenate([-sin, sin], axis=1), (1, reps))
    return cos_t, sin_t


def _bias_ruler(table):
    nh = table.shape[0]
    far_past = jnp.broadcast_to(table[:, 2 * MAX_REL:], (nh, BAND_ROWS - MAX_REL))
    far_future = jnp.broadcast_to(table[:, :1], (nh, BIAS_RULER - ATT_BQ - BAND_ROWS - MAX_REL - 1))
    wrapped = jnp.broadcast_to(table[:, 2 * MAX_REL:], (nh, ATT_BQ))
    ruler = jnp.concatenate([far_past, table[:, ::-1], far_future, wrapped], axis=1)
    return ruler.astype(F32).reshape(nh, 1, BIAS_RULER)


def kernel(x_prompt, x_sample, cache_a_k, cache_a_v, cache_b_k, cache_b_v, ffn1_norm, ffn1_w_gate, ffn1_w_up, ffn1_w_down, mix_norm, w_in, lambda_q1, lambda_k1, lambda_q2, lambda_k2, subln_a, rel_bias_b, w_branch_a, w_branch_b, w_out, ffn2_norm, ffn2_w_gate, ffn2_w_up, ffn2_w_down, final_norm):
    batch, seq, _ = x_prompt.shape
    dec_batch, dec_seq, _ = x_sample.shape
    depth = ffn1_norm.shape[0]
    past_a = cache_a_k.shape[2]
    past_b = cache_b_k.shape[2]
    assert seq % max(ATT_BQ, PROJ_BM) == 0 and seq >= BAND_ROWS
    assert dec_seq == CHUNK and past_b == BAND_ROWS and PROJ_BM % dec_seq == 0

    xp = x_prompt.reshape(batch * seq, D_MODEL)
    xs = x_sample.reshape(dec_batch * dec_seq, D_MODEL)
    rope_p = _rope_tables(jnp.arange(seq))
    rope_s = _rope_tables(past_a + (jnp.arange(PROJ_BM) % dec_seq))
    cache_a_k2 = cache_a_k.reshape(depth, dec_batch, past_a * H_A, HEAD_LANES)
    cache_a_v2 = cache_a_v.reshape(depth, dec_batch, past_a * H_A, HEAD_LANES)
    cache_b_k2 = cache_b_k.reshape(depth, dec_batch, past_b * H_B, HEAD_LANES)
    cache_b_v2 = cache_b_v.reshape(depth, dec_batch, past_b * H_B, HEAD_LANES)

    outs = {k: [] for k in ("akp", "avp", "bkp", "bvp", "aks", "avs", "bks", "bvs")}
    for layer in range(depth):
        lam_init = 0.8 - 0.6 * math.exp(-0.3 * layer)
        lam = (jnp.exp(jnp.sum(lambda_q1[layer].astype(F32) * lambda_k1[layer].astype(F32)))
               - jnp.exp(jnp.sum(lambda_q2[layer].astype(F32) * lambda_k2[layer].astype(F32)))
               + lam_init).reshape(1)
        lam_scale = 1.0 - lam_init
        g1 = ffn1_norm[layer].reshape(1, D_MODEL)
        gm = mix_norm[layer].reshape(1, D_MODEL)
        g2 = ffn2_norm[layer].reshape(1, D_MODEL)
        gf = final_norm.reshape(1, D_MODEL)
        subln = subln_a[layer].reshape(1, DV_A)
        w1g, w1u, w1d = (w[layer].astype(BF16) for w in (ffn1_w_gate, ffn1_w_up, ffn1_w_down))
        table = rel_bias_b[layer]
        ruler = _bias_ruler(table)
        last = layer == depth - 1

        later = [w[layer] for w in (ffn2_w_gate, ffn2_w_up, ffn2_w_down, w_in, w_branch_a, w_branch_b, w_out)]
        x1_p, hn_p, w2g, w2u, w2d, win, wa, wb, wo = _ffn(xp, g1, w1g, w1u, w1d, gm, "mix", casts=later)
        x1_s, hn_s = _ffn(xs, g1, w1g, w1u, w1d, gm, "mix")

        def trunk(x1, hn, rope_tabs, rope_period, attend, transposed_a):
            c = 0
            qa = _proj(hn, win, c, W_QK_A, "rope_q", rope_tabs, rope_period, transposed=transposed_a); c += W_QK_A
            ka, ka16 = _proj(hn, win, c, W_QK_A, "rope_kv", rope_tabs, rope_period); c += W_QK_A
            va, va16 = _proj(hn, win, c, W_A, "kv", transposed=transposed_a); c += W_A
            qb = _proj(hn, win, c, W_B, "plain"); c += W_B
            kb, kb16 = _proj(hn, win, c, W_B, "kv"); c += W_B
            vb, vb16 = _proj(hn, win, c, W_B, "kv"); c += W_B
            sga = _proj(hn, win, c, D_MODEL, "sigmoid"); c += D_MODEL
            sgb = _proj(hn, win, c, D_MODEL, "sigmoid")
            oa, ob = attend(qa, ka16, va16, qb, kb16, vb16)
            x2 = _merge(x1, oa, ob, sga, sgb, wa, wb, wo)
            xo = _ffn(x2, g2, w2g, w2u, w2d, gf, "final" if last else "mix")[0]
            return xo, ka, va, kb, vb

        def attend_prompt(qa, ka16, va16, qb, kb16, vb16):
            oa = _attn_a_prompt(lam, qa, ka16, va16, subln, batch, seq, lam_scale)
            ob = _attn_b_prompt(qb, kb16, vb16, ruler, batch, seq)
            return oa, ob

        def attend_sample(qa, ka16, va16, qb, kb16, vb16):
            oa = _attn_a_sample(lam, qa, cache_a_k2, cache_a_v2, ka16, va16, subln, layer, lam_scale)
            ob = _attn_b_sample(qb, cache_b_k2, cache_b_v2, kb16, vb16, ruler, layer)
            return oa, ob

        xp, ka, va, kb, vb = trunk(x1_p, hn_p, rope_p, seq, attend_prompt, True)
        outs["akp"].append(ka.reshape(batch, seq, H_A, 2 * DK_A))
        outs["avp"].append(va.reshape(batch, seq, H_A, DV_A))
        rows = min(BAND_ROWS, seq)
        outs["bkp"].append(kb.reshape(batch, seq, H_B, DH_B)[:, seq - rows:])
        outs["bvp"].append(vb.reshape(batch, seq, H_B, DH_B)[:, seq - rows:])

        xs, ka, va, kb, vb = trunk(x1_s, hn_s, rope_s, PROJ_BM, attend_sample, False)
        outs["aks"].append(ka.reshape(dec_batch, dec_seq, H_A, 2 * DK_A))
        outs["avs"].append(va.reshape(dec_batch, dec_seq, H_A, DV_A))
        kb_all = jnp.concatenate([cache_b_k[layer], kb.reshape(dec_batch, dec_seq, H_B, DH_B)], axis=1)
        vb_all = jnp.concatenate([cache_b_v[layer], vb.reshape(dec_batch, dec_seq, H_B, DH_B)], axis=1)
        outs["bks"].append(kb_all[:, dec_seq:])
        outs["bvs"].append(vb_all[:, dec_seq:])

    y_prompt = xp.reshape(batch, seq, D_MODEL)
    y_sample = xs.reshape(dec_batch, dec_seq, D_MODEL)
    return (y_prompt, y_sample,
            jnp.stack(outs["akp"]), jnp.stack(outs["avp"]), jnp.stack(outs["bkp"]), jnp.stack(outs["bvp"]),
            jnp.stack(outs["aks"]), jnp.stack(outs["avs"]), jnp.stack(outs["bks"]), jnp.stack(outs["bvs"]))
```

```python
import functools
import math

import jax
import jax.numpy as jnp
from jax import lax
from jax.experimental import pallas as pl
from jax.experimental.pallas import tpu as pltpu

F32 = jnp.float32
BF16 = jnp.bfloat16

D_MODEL = 2048
CHUNK = 64
H_A = 8
DK_A = 64
DV_A = 2 * DK_A
W_QK_A = H_A * 2 * DK_A
W_A = H_A * DV_A
H_B = 8
DH_B = 128
W_B = H_B * DH_B
N_PREV = 8
BAND_ROWS = N_PREV * CHUNK
MAX_REL = 128
FFN_DIM = 4 * D_MODEL
ROPE_THETA = 10000.0
EPS = 1e-6
NEG = -1e30

HEAD_LANES = 128
BF16_SUBLANES = 16
VMEM_LIMIT = 58 * 1024 * 1024

FFN_BM = 512
FFN_BF = 1024
PROJ_BM = 1024
PROJ_BN = 1024
MERGE_BM = 512
ATT_BQ = 256
BAND_HEADS_PER_STEP = 4
BAND_WIN = BAND_ROWS + ATT_BQ
BIAS_RULER = BAND_WIN + ATT_BQ


def _params(*sem):
    return pltpu.CompilerParams(dimension_semantics=sem, vmem_limit_bytes=VMEM_LIMIT)


def _rms(x, g):
    return x * lax.rsqrt(jnp.mean(x * x, axis=-1, keepdims=True) + EPS) * g


def _ffn_kernel(x_ref, g_ref, wg_ref, wu_ref, wd_ref, g2_ref, *rest, mode, ncast):
    cast_in, rest = rest[:ncast], rest[ncast:]
    if mode == "mix":
        o_ref, hn_ref, *rest = rest
    else:
        o_ref, *rest = rest
    cast_out, (h_ref,) = rest[:ncast], rest[ncast:]
    f = pl.program_id(1)

    @pl.when(f == 0)
    def _():
        x = x_ref[...]
        h_ref[...] = _rms(x, g_ref[...]).astype(BF16)
        o_ref[...] = x

    h = h_ref[...]
    a = jnp.dot(h, wg_ref[...], preferred_element_type=F32)
    b = jnp.dot(h, wu_ref[...], preferred_element_type=F32)
    t = (0.5 * (a * jax.nn.sigmoid(a)) * b).astype(BF16)
    o_ref[...] += jnp.dot(t, wd_ref[...], preferred_element_type=F32)

    for src, dst in zip(cast_in, cast_out):
        dst[...] = src[...].astype(BF16)

    @pl.when(f == pl.num_programs(1) - 1)
    def _():
        y = _rms(o_ref[...], g2_ref[...])
        if mode == "mix":
            hn_ref[...] = y.astype(BF16)
        else:
            o_ref[...] = y


def _ffn(x, g, wg, wu, wd, g2, mode, casts=()):
    m = x.shape[0]
    bm, bf = FFN_BM, FFN_BF
    grid = (m // bm, FFN_DIM // bf)
    nf = grid[1]
    row = pl.BlockSpec((bm, D_MODEL), lambda i, f: (i, 0))
    vec = pl.BlockSpec((1, D_MODEL), lambda i, f: (0, 0))
    in_specs = [
        row, vec,
        pl.BlockSpec((D_MODEL, bf), lambda i, f: (0, f)),
        pl.BlockSpec((D_MODEL, bf), lambda i, f: (0, f)),
        pl.BlockSpec((bf, D_MODEL), lambda i, f: (f, 0)),
        vec,
    ]
    if mode == "mix":
        out_shape = [jax.ShapeDtypeStruct((m, D_MODEL), F32), jax.ShapeDtypeStruct((m, D_MODEL), BF16)]
        out_specs = [row, row]
    else:
        out_shape = [jax.ShapeDtypeStruct((m, D_MODEL), F32)]
        out_specs = [row]
    nsteps = grid[0] * nf
    for w in casts:
        rows, cols = w.shape
        if rows % (nsteps * BF16_SUBLANES) == 0:
            block_rows, hold = rows // nsteps, 1
        else:
            block_rows, hold = BF16_SUBLANES, nsteps * BF16_SUBLANES // rows
            assert rows * hold == nsteps * BF16_SUBLANES
        spec = pl.BlockSpec((block_rows, cols), functools.partial(lambda i, f, hold: ((i * nf + f) // hold, 0), hold=hold))
        in_specs.append(spec)
        out_specs.append(spec)
        out_shape.append(jax.ShapeDtypeStruct(w.shape, BF16))
    return pl.pallas_call(
        functools.partial(_ffn_kernel, mode=mode, ncast=len(casts)),
        grid=grid, in_specs=in_specs, out_specs=out_specs, out_shape=out_shape,
        scratch_shapes=[pltpu.VMEM((bm, D_MODEL), BF16)],
        compiler_params=_params("parallel", "arbitrary"),
        name="ffn_" + mode,
    )(x, g, wg, wu, wd, g2, *casts)


def _rope_slabs(z, cos, sin_signed):
    lane = lax.broadcasted_iota(jnp.int32, cos.shape, 1)
    first_half = (lane % DK_A) < (DK_A // 2)
    outs = []
    for s in range(z.shape[1] // HEAD_LANES):
        slab = z[:, s * HEAD_LANES:(s + 1) * HEAD_LANES]
        partner = jnp.where(first_half,
                            pltpu.roll(slab, HEAD_LANES - DK_A // 2, 1),
                            pltpu.roll(slab, DK_A // 2, 1))
        outs.append(slab * cos + partner * sin_signed)
    return jnp.concatenate(outs, axis=1)


def _proj_kernel(h_ref, w_ref, *rest, epilogue, transposed):
    rope = epilogue in ("rope_q", "rope_kv")
    heads = epilogue in ("rope_kv", "kv")
    if rope:
        cos_ref, sin_ref, *rest = rest
    z = jnp.dot(h_ref[...], w_ref[...], preferred_element_type=F32)
    if rope:
        z = _rope_slabs(z, cos_ref[...], sin_ref[...])
    if epilogue == "rope_q":
        z = z * (DK_A ** -0.5)
    elif epilogue == "sigmoid":
        z = jax.nn.sigmoid(z)
    nh = z.shape[1] // HEAD_LANES
    if heads:
        o_ref, o16_ref = rest
        for h in range(nh):
            zh = z[:, h * HEAD_LANES:(h + 1) * HEAD_LANES]
            o_ref[pl.ds(h, z.shape[0], stride=nh), :] = zh
            o16_ref[h] = (zh.T if transposed else zh).astype(BF16)
    elif transposed:
        (o_ref,) = rest
        for h in range(nh):
            o_ref[h] = z[:, h * HEAD_LANES:(h + 1) * HEAD_LANES].T.astype(BF16)
    else:
        (o_ref,) = rest
        o_ref[...] = z.astype(o_ref.dtype)


def _proj(hn, w_in, col0, ncols, epilogue, rope_tabs=None, rope_period=None, transposed=False):
    m = hn.shape[0]
    bm, bn = PROJ_BM, PROJ_BN
    joff = col0 // bn
    nh = bn // HEAD_LANES
    head_major = (pl.BlockSpec((nh, HEAD_LANES, bm), lambda i, j: (0, 0, i)) if transposed
                  else pl.BlockSpec((nh, bm, HEAD_LANES), lambda i, j: (0, i, 0)))
    head_major_shape = jax.ShapeDtypeStruct((nh, HEAD_LANES, m) if transposed else (nh, m, HEAD_LANES), BF16)
    in_specs = [
        pl.BlockSpec((bm, D_MODEL), lambda i, j: (i, 0)),
        pl.BlockSpec((D_MODEL, bn), lambda i, j: (0, j + joff)),
    ]
    args = [hn, w_in]
    if rope_tabs is not None:
        nper = rope_period // bm
        tab = pl.BlockSpec((bm, HEAD_LANES), lambda i, j: (i % nper, 0))
        in_specs += [tab, tab]
        args += list(rope_tabs)
    if epilogue in ("rope_kv", "kv"):
        nseg = ncols // bn
        copy_block = (nh, HEAD_LANES, bm) if transposed else (nh, bm, HEAD_LANES)
        copy_index = (lambda i, j: (j, 0, 0, i)) if transposed else (lambda i, j: (j, 0, i, 0))
        out_specs = (pl.BlockSpec((None, bm * nh, HEAD_LANES), lambda i, j: (j, i, 0)),
                     pl.BlockSpec((None,) + copy_block, copy_index))
        out_shape = (jax.ShapeDtypeStruct((nseg, m * nh, HEAD_LANES), F32),
                     jax.ShapeDtypeStruct((nseg,) + head_major_shape.shape, BF16))
    elif transposed:
        assert ncols == bn
        out_specs, out_shape = head_major, head_major_shape
    else:
        out_specs = pl.BlockSpec((bm, bn), lambda i, j: (i, j))
        out_shape = jax.ShapeDtypeStruct((m, ncols), BF16)
    return pl.pallas_call(
        functools.partial(_proj_kernel, epilogue=epilogue, transposed=transposed),
        grid=(m // bm, ncols // bn), in_specs=in_specs,
        out_specs=out_specs, out_shape=out_shape,
        compiler_params=_params("parallel", "arbitrary"),
        name="proj_" + epilogue,
    )(*args)


def _block_diag_q(q):
    lane = lax.broadcasted_iota(jnp.int32, q.shape, 1)
    zero = jnp.zeros_like(q)
    return jnp.concatenate([jnp.where(lane < DK_A, q, zero), jnp.where(lane >= DK_A, q, zero)], axis=0)


def _qk(q, k):
    return lax.dot_general(q, k, (((1,), (1,)), ((), ())), preferred_element_type=F32)


def _attn_a_prompt_kernel(lam_ref, qt_ref, k_ref, vt_ref, subln_ref, o_ref, *, lam_scale):
    qi = pl.program_id(1)
    nh, _, bq = qt_ref.shape
    row = lax.broadcasted_iota(jnp.int32, (HEAD_LANES, bq), 0)
    qbd = []
    for h in range(nh):
        qt = qt_ref[h]
        zero = jnp.zeros_like(qt)
        qbd.append(jnp.concatenate([jnp.where(row < DK_A, qt, zero), jnp.where(row >= DK_A, qt, zero)], axis=1))

    def step(j, carries, masked):
        start = pl.multiple_of(j * bq, bq)
        scores = [jnp.dot(k_ref[h, pl.ds(start, bq), :], qbd[h], preferred_element_type=F32) for h in range(nh)]
        if masked:
            kr = lax.broadcasted_iota(jnp.int32, (bq, 2 * bq), 0)
            qc = lax.broadcasted_iota(jnp.int32, (bq, 2 * bq), 1) % bq
            visible = (kr // CHUNK) <= (qc // CHUNK)
        out = []
        for h in range(nh):
            m, l, acc = carries[h]
            s = scores[h]
            if masked:
                s = jnp.where(visible, s, NEG)
            m_new = jnp.maximum(m, jnp.max(s, axis=0, keepdims=True))
            alpha = jnp.exp(m - m_new)
            p = jnp.exp(s - m_new)
            l = alpha * l + jnp.sum(p, axis=0, keepdims=True)
            acc = alpha * acc + jnp.dot(vt_ref[h, :, pl.ds(start, bq)], p.astype(BF16),
                                        preferred_element_type=F32)
            out.append((m_new, l, acc))
        return tuple(out)

    init = (jnp.full((1, 2 * bq), -jnp.inf, F32), jnp.zeros((1, 2 * bq), F32),
            jnp.zeros((HEAD_LANES, 2 * bq), F32))
    carries = lax.fori_loop(0, qi, functools.partial(step, masked=False), (init,) * nh)
    carries = step(qi, carries, masked=True)
    for h in range(nh):
        _, l, acc = carries[h]
        o = acc / l
        o = (o[:, :bq] - lam_ref[0] * o[:, bq:]).T
        o = _rms(o, subln_ref[...]) * lam_scale
        o_ref[:, h * HEAD_LANES:(h + 1) * HEAD_LANES] = o.astype(o_ref.dtype)


def _attn_a_prompt(lam, qt, k16, vt, subln, batch, seq, lam_scale):
    bq = ATT_BQ
    nq = seq // bq
    return pl.pallas_call(
        functools.partial(_attn_a_prompt_kernel, lam_scale=lam_scale),
        grid=(batch, nq),
        in_specs=[
            pl.BlockSpec(memory_space=pltpu.SMEM),
            pl.BlockSpec((H_A, HEAD_LANES, bq), lambda b, i: (0, 0, b * nq + i)),
            pl.BlockSpec((H_A, seq, HEAD_LANES), lambda b, i: (0, b, 0)),
            pl.BlockSpec((H_A, HEAD_LANES, seq), lambda b, i: (0, 0, b)),
            pl.BlockSpec((1, DV_A), lambda b, i: (0, 0)),
        ],
        out_specs=pl.BlockSpec((bq, W_A), lambda b, i: (b * nq + i, 0)),
        out_shape=jax.ShapeDtypeStruct((batch * seq, W_A), BF16),
        compiler_params=_params("parallel", "arbitrary"),
        name="attn_a_prompt",
    )(lam, qt, k16, vt, subln)


def _softmax_two_parts(sc, sn, vc, vn):
    m = jnp.maximum(jnp.max(sc, axis=-1, keepdims=True), jnp.max(sn, axis=-1, keepdims=True))
    pc = jnp.exp(sc - m)
    pn = jnp.exp(sn - m)
    l = jnp.sum(pc, axis=-1, keepdims=True) + jnp.sum(pn, axis=-1, keepdims=True)
    acc = (jnp.dot(pc.astype(BF16), vc, preferred_element_type=F32)
           + jnp.dot(pn.astype(BF16), vn, preferred_element_type=F32))
    return acc, l


def _tn(a, b):
    return lax.dot_general(a, b, (((0,), (0,)), ((), ())), preferred_element_type=F32)


def _attn_a_sample_kernel(lam_ref, q_ref, kc_ref, vc_ref, kn_ref, vn_ref, subln_ref, o_ref, *, lam_scale):
    n = q_ref.shape[0]
    past = kc_ref.shape[0] // H_A
    scores = []
    for h in range(H_A):
        qbd_t = _block_diag_q(q_ref[:, h * HEAD_LANES:(h + 1) * HEAD_LANES]).astype(F32).T.astype(BF16)
        kc = kc_ref[pl.ds(h, past, stride=H_A), :].astype(BF16)
        scores.append((jnp.dot(kc, qbd_t, preferred_element_type=F32),
                       jnp.dot(kn_ref[h], qbd_t, preferred_element_type=F32)))
    for h in range(H_A):
        sc, sn = scores[h]
        m = jnp.maximum(jnp.max(sc, axis=0, keepdims=True), jnp.max(sn, axis=0, keepdims=True))
        pc = jnp.exp(sc - m)
        pn = jnp.exp(sn - m)
        l = jnp.sum(pc, axis=0, keepdims=True) + jnp.sum(pn, axis=0, keepdims=True)
        acc = (_tn(vc_ref[pl.ds(h, past, stride=H_A), :].astype(BF16), pc.astype(BF16))
               + _tn(vn_ref[h], pn.astype(BF16)))
        o = acc / l
        o = (o[:, :n] - lam_ref[0] * o[:, n:]).T
        o = _rms(o, subln_ref[...]) * lam_scale
        o_ref[:, h * HEAD_LANES:(h + 1) * HEAD_LANES] = o.astype(o_ref.dtype)


def _attn_a_sample(lam, qa, cache_k, cache_v, k16, v16, subln, layer, lam_scale):
    _, nb, cache_rows, _ = cache_k.shape
    n = qa.shape[0] // nb
    rows = pl.BlockSpec((n, W_A), lambda b: (b, 0))
    new = pl.BlockSpec((H_A, n, HEAD_LANES), lambda b: (0, b, 0))
    old = pl.BlockSpec((None, None, cache_rows, HEAD_LANES), lambda b: (layer, b, 0, 0))
    return pl.pallas_call(
        functools.partial(_attn_a_sample_kernel, lam_scale=lam_scale),
        grid=(nb,),
        in_specs=[pl.BlockSpec(memory_space=pltpu.SMEM), rows, old, old, new, new,
                  pl.BlockSpec((1, DV_A), lambda b: (0, 0))],
        out_specs=rows,
        out_shape=jax.ShapeDtypeStruct(qa.shape, BF16),
        compiler_params=_params("parallel"),
        name="attn_a_sample",
    )(lam, qa, cache_k, cache_v, k16, v16, subln)


def _toeplitz_bias(ruler_row, nq, nk):
    rows = jnp.broadcast_to(ruler_row, (nq, ruler_row.shape[1]))
    return pltpu.roll(rows, 0, 1, stride=1, stride_axis=0)[:, :nk]


def _attn_b_prompt_kernel(q_ref, k_ref, v_ref, ruler_ref, o_ref, kb_ref, vb_ref, bias_ref):
    qi = pl.program_id(2)
    bq = q_ref.shape[0]
    ng, seq, _ = k_ref.shape

    @pl.when(qi == 0)
    def _():
        zeros = jnp.zeros((BAND_ROWS, HEAD_LANES), BF16)
        d = (lax.broadcasted_iota(jnp.int32, (bq, BAND_WIN), 1) // CHUNK
             - lax.broadcasted_iota(jnp.int32, (bq, BAND_WIN), 0) // CHUNK)
        for g in range(ng):
            kb_ref[g, 0:BAND_ROWS, :] = zeros
            vb_ref[g, 0:BAND_ROWS, :] = zeros
            kb_ref[g, BAND_ROWS:BAND_ROWS + seq, :] = k_ref[g]
            vb_ref[g, BAND_ROWS:BAND_ROWS + seq, :] = v_ref[g]
            bias = _toeplitz_bias(ruler_ref[g], bq, BAND_WIN)
            bias_ref[g] = jnp.where(d >= 0, jnp.where(d <= N_PREV, bias, NEG), NEG)

    start = pl.multiple_of(qi * bq, bq)
    win = pl.ds(start, BAND_WIN)
    scores = [_qk(q_ref[:, g * HEAD_LANES:(g + 1) * HEAD_LANES], kb_ref[g, win, :]) for g in range(ng)]
    w = lax.broadcasted_iota(jnp.int32, (bq, BAND_WIN), 1)
    in_sequence = w + qi * bq >= BAND_ROWS
    for g in range(ng):
        s = scores[g] * (DH_B ** -0.5) + bias_ref[g]
        s = jnp.where(in_sequence, s, NEG)
        p = jnp.exp(s - jnp.max(s, axis=-1, keepdims=True))
        l = jnp.sum(p, axis=-1, keepdims=True)
        o = jnp.dot(p.astype(BF16), vb_ref[g, win, :], preferred_element_type=F32) / l
        o_ref[:, g * HEAD_LANES:(g + 1) * HEAD_LANES] = o.astype(o_ref.dtype)


def _attn_b_prompt(qb, kv16, ruler, batch, seq):
    bq, ng = ATT_BQ, BAND_HEADS_PER_STEP
    nq = seq // bq
    k16 = pl.BlockSpec((None, ng, seq, HEAD_LANES), lambda b, h, i: (0, h, b, 0))
    v16 = pl.BlockSpec((None, ng, seq, HEAD_LANES), lambda b, h, i: (1, h, b, 0))
    rows = pl.BlockSpec((bq, ng * HEAD_LANES), lambda b, h, i: (b * nq + i, h))
    return pl.pallas_call(
        _attn_b_prompt_kernel,
        grid=(batch, H_B // ng, nq),
        in_specs=[rows, k16, v16, pl.BlockSpec((ng, 1, BIAS_RULER), lambda b, h, i: (h, 0, 0))],
        out_specs=rows,
        out_shape=jax.ShapeDtypeStruct((batch * seq, W_B), BF16),
        scratch_shapes=[pltpu.VMEM((ng, BAND_ROWS + seq, HEAD_LANES), BF16),
                        pltpu.VMEM((ng, BAND_ROWS + seq, HEAD_LANES), BF16),
                        pltpu.VMEM((ng, bq, BAND_WIN), F32)],
        compiler_params=_params("parallel", "parallel", "arbitrary"),
        name="attn_b_prompt",
    )(qb, kv16, kv16, ruler)


def _attn_b_sample_kernel(q_ref, kc_ref, vc_ref, kn_ref, vn_ref, ruler_ref, o_ref, bias_ref):
    n = q_ref.shape[0]
    past = kc_ref.shape[0] // H_B
    scale = DH_B ** -0.5

    @pl.when(pl.program_id(0) == 0)
    def _():
        for h in range(H_B):
            bias_ref[h] = _toeplitz_bias(ruler_ref[h], n, past + n)

    scores = []
    for h in range(H_B):
        q = q_ref[:, h * HEAD_LANES:(h + 1) * HEAD_LANES]
        scores.append((_qk(q, kc_ref[pl.ds(h, past, stride=H_B), :].astype(BF16)), _qk(q, kn_ref[h])))
    for h in range(H_B):
        sc = scores[h][0] * scale + bias_ref[h, :, :past]
        sn = scores[h][1] * scale + bias_ref[h, :, past:]
        acc, l = _softmax_two_parts(sc, sn, vc_ref[pl.ds(h, past, stride=H_B), :].astype(BF16), vn_ref[h])
        o_ref[:, h * HEAD_LANES:(h + 1) * HEAD_LANES] = (acc / l).astype(o_ref.dtype)


def _attn_b_sample(qb, cache_k, cache_v, kv16, ruler, layer):
    _, nb, cache_rows, _ = cache_k.shape
    past = cache_rows // H_B
    n = qb.shape[0] // nb
    rows = pl.BlockSpec((n, W_B), lambda b: (b, 0))
    new_k = pl.BlockSpec((None, H_B, n, HEAD_LANES), lambda b: (0, 0, b, 0))
    new_v = pl.BlockSpec((None, H_B, n, HEAD_LANES), lambda b: (1, 0, b, 0))
    old = pl.BlockSpec((None, None, cache_rows, HEAD_LANES), lambda b: (layer, b, 0, 0))
    return pl.pallas_call(
        _attn_b_sample_kernel,
        grid=(nb,),
        in_specs=[rows, old, old, new_k, new_v,
                  pl.BlockSpec((H_B, 1, BIAS_RULER), lambda b: (0, 0, 0))],
        out_specs=rows,
        out_shape=jax.ShapeDtypeStruct(qb.shape, BF16),
        scratch_shapes=[pltpu.VMEM((H_B, n, past + n), F32)],
        compiler_params=_params("arbitrary"),
        name="attn_b_sample",
    )(qb, cache_k, cache_v, kv16, kv16, ruler)


def _merge_kernel(x_ref, oa_ref, ob_ref, sga_ref, sgb_ref, wa_ref, wb_ref, wo_ref, o_ref):
    a = jnp.dot(oa_ref[...], wa_ref[...], preferred_element_type=F32)
    b = jnp.dot(ob_ref[...], wb_ref[...], preferred_element_type=F32)
    merged = sga_ref[...].astype(F32) * a + sgb_ref[...].astype(F32) * b
    o_ref[...] = x_ref[...] + jnp.dot(merged.astype(BF16), wo_ref[...], preferred_element_type=F32)


def _merge(x, oa, ob, sg, wa, wb, wo):
    m = x.shape[0]
    bm = MERGE_BM
    wide = pl.BlockSpec((bm, D_MODEL), lambda i: (i, 0))
    gate_b = pl.BlockSpec((bm, D_MODEL), lambda i: (i, 1))
    half = pl.BlockSpec((bm, W_A), lambda i: (i, 0))

    def whole(shape):
        return pl.BlockSpec(shape, lambda i: (0, 0), pipeline_mode=pl.Buffered(1))

    return pl.pallas_call(
        _merge_kernel,
        grid=(m // bm,),
        in_specs=[wide, half, half, wide, gate_b,
                  whole((W_A, D_MODEL)), whole((W_B, D_MODEL)), whole((D_MODEL, D_MODEL))],
        out_specs=wide,
        out_shape=jax.ShapeDtypeStruct((m, D_MODEL), F32),
        compiler_params=_params("parallel"),
        name="merge",
    )(x, oa, ob, sg, sg, wa, wb, wo)


def _rope_tables(pos):
    half = DK_A // 2
    inv = ROPE_THETA ** (-jnp.arange(half, dtype=F32) / half)
    ang = pos.astype(F32)[:, None] * inv[None, :]
    cos, sin = jnp.cos(ang), jnp.sin(ang)
    reps = HEAD_LANES // DK_A
    cos_t = jnp.tile(jnp.concatenate([cos, cos], axis=1), (1, reps))
    sin_t = jnp.tile(jnp.concatenate([-sin, sin], axis=1), (1, reps))
    return cos_t, sin_t


def _bias_ruler(table):
    nh = table.shape[0]
    far_past = jnp.broadcast_to(table[:, 2 * MAX_REL:], (nh, BAND_ROWS - MAX_REL))
    far_future = jnp.broadcast_to(table[:, :1], (nh, BIAS_RULER - ATT_BQ - BAND_ROWS - MAX_REL - 1))
    wrapped = jnp.broadcast_to(table[:, 2 * MAX_REL:], (nh, ATT_BQ))
    ruler = jnp.concatenate([far_past, table[:, ::-1], far_future, wrapped], axis=1)
    return ruler.astype(F32).reshape(nh, 1, BIAS_RULER)


def kernel(x_prompt, x_sample, cache_a_k, cache_a_v, cache_b_k, cache_b_v, ffn1_norm, ffn1_w_gate, ffn1_w_up, ffn1_w_down, mix_norm, w_in, lambda_q1, lambda_k1, lambda_q2, lambda_k2, subln_a, rel_bias_b, w_branch_a, w_branch_b, w_out, ffn2_norm, ffn2_w_gate, ffn2_w_up, ffn2_w_down, final_norm):
    batch, seq, _ = x_prompt.shape
    dec_batch, dec_seq, _ = x_sample.shape
    depth = ffn1_norm.shape[0]
    past_a = cache_a_k.shape[2]
    past_b = cache_b_k.shape[2]
    assert seq % max(ATT_BQ, PROJ_BM) == 0 and seq >= BAND_ROWS
    assert dec_seq == CHUNK and past_b == BAND_ROWS and PROJ_BM % dec_seq == 0

    xp = x_prompt.reshape(batch * seq, D_MODEL)
    xs = x_sample.reshape(dec_batch * dec_seq, D_MODEL)
    rope_p = _rope_tables(jnp.arange(seq))
    rope_s = _rope_tables(past_a + (jnp.arange(PROJ_BM) % dec_seq))
    cache_a_k2 = cache_a_k.reshape(depth, dec_batch, past_a * H_A, HEAD_LANES)
    cache_a_v2 = cache_a_v.reshape(depth, dec_batch, past_a * H_A, HEAD_LANES)
    cache_b_k2 = cache_b_k.reshape(depth, dec_batch, past_b * H_B, HEAD_LANES)
    cache_b_v2 = cache_b_v.reshape(depth, dec_batch, past_b * H_B, HEAD_LANES)

    outs = {k: [] for k in ("akp", "avp", "bkp", "bvp", "aks", "avs", "bks", "bvs")}
    for layer in range(depth):
        lam_init = 0.8 - 0.6 * math.exp(-0.3 * layer)
        lam = (jnp.exp(jnp.sum(lambda_q1[layer].astype(F32) * lambda_k1[layer].astype(F32)))
               - jnp.exp(jnp.sum(lambda_q2[layer].astype(F32) * lambda_k2[layer].astype(F32)))
               + lam_init).reshape(1)
        lam_scale = 1.0 - lam_init
        g1 = ffn1_norm[layer].reshape(1, D_MODEL)
        gm = mix_norm[layer].reshape(1, D_MODEL)
        g2 = ffn2_norm[layer].reshape(1, D_MODEL)
        gf = final_norm.reshape(1, D_MODEL)
        subln = subln_a[layer].reshape(1, DV_A)
        w1g, w1u, w1d = (w[layer].astype(BF16) for w in (ffn1_w_gate, ffn1_w_up, ffn1_w_down))
        table = rel_bias_b[layer]
        ruler = _bias_ruler(table)
        last = layer == depth - 1

        later = [w[layer] for w in (ffn2_w_gate, ffn2_w_up, ffn2_w_down, w_in, w_branch_a, w_branch_b, w_out)]
        x1_p, hn_p, w2g, w2u, w2d, win, wa, wb, wo = _ffn(xp, g1, w1g, w1u, w1d, gm, "mix", casts=later)
        x1_s, hn_s = _ffn(xs, g1, w1g, w1u, w1d, gm, "mix")

        def trunk(x1, hn, rope_tabs, rope_period, attend, transposed_a):
            c = 0
            qa = _proj(hn, win, c, W_QK_A, "rope_q", rope_tabs, rope_period, transposed=transposed_a); c += W_QK_A
            ka, ka16 = _proj(hn, win, c, W_QK_A, "rope_kv", rope_tabs, rope_period); c += W_QK_A
            va, va16 = _proj(hn, win, c, W_A, "kv", transposed=transposed_a); c += W_A
            qb = _proj(hn, win, c, W_B, "plain"); c += W_B
            kvb, kvb16 = _proj(hn, win, c, 2 * W_B, "kv"); c += 2 * W_B
            sg = _proj(hn, win, c, 2 * D_MODEL, "sigmoid")
            oa, ob = attend(qa, ka16[0], va16[0], qb, kvb16)
            x2 = _merge(x1, oa, ob, sg, wa, wb, wo)
            xo = _ffn(x2, g2, w2g, w2u, w2d, gf, "final" if last else "mix")[0]
            return xo, ka[0], va[0], kvb[0], kvb[1]

        def attend_prompt(qa, ka16, va16, qb, kvb16):
            oa = _attn_a_prompt(lam, qa, ka16, va16, subln, batch, seq, lam_scale)
            ob = _attn_b_prompt(qb, kvb16, ruler, batch, seq)
            return oa, ob

        def attend_sample(qa, ka16, va16, qb, kvb16):
            oa = _attn_a_sample(lam, qa, cache_a_k2, cache_a_v2, ka16, va16, subln, layer, lam_scale)
            ob = _attn_b_sample(qb, cache_b_k2, cache_b_v2, kvb16, ruler, layer)
            return oa, ob

        xp, ka, va, kb, vb = trunk(x1_p, hn_p, rope_p, seq, attend_prompt, True)
        outs["akp"].append(ka.reshape(batch, seq, H_A, 2 * DK_A))
        outs["avp"].append(va.reshape(batch, seq, H_A, DV_A))
        rows = min(BAND_ROWS, seq)
        outs["bkp"].append(kb.reshape(batch, seq, H_B, DH_B)[:, seq - rows:])
        outs["bvp"].append(vb.reshape(batch, seq, H_B, DH_B)[:, seq - rows:])

        xs, ka, va, kb, vb = trunk(x1_s, hn_s, rope_s, PROJ_BM, attend_sample, False)
        outs["aks"].append(ka.reshape(dec_batch, dec_seq, H_A, 2 * DK_A))
        outs["avs"].append(va.reshape(dec_batch, dec_seq, H_A, DV_A))
        kb_all = jnp.concatenate([cache_b_k[layer], kb.reshape(dec_batch, dec_seq, H_B, DH_B)], axis=1)
        vb_all = jnp.concatenate([cache_b_v[layer], vb.reshape(dec_batch, dec_seq, H_B, DH_B)], axis=1)
        outs["bks"].append(kb_all[:, dec_seq:])
        outs["bvs"].append(vb_all[:, dec_seq:])

    y_prompt = xp.reshape(batch, seq, D_MODEL)
    y_sample = xs.reshape(dec_batch, dec_seq, D_MODEL)
    return (y_prompt, y_sample,
            jnp.stack(outs["akp"]), jnp.stack(outs["avp"]), jnp.stack(outs["bkp"]), jnp.stack(outs["bvp"]),
            jnp.stack(outs["aks"]), jnp.stack(outs["avs"]), jnp.stack(outs["bks"]), jnp.stack(outs["bvs"]))
```

```python
import functools
import math

import jax
import jax.numpy as jnp
from jax import lax
from jax.experimental import pallas as pl
from jax.experimental.pallas import tpu as pltpu

F32 = jnp.float32
BF16 = jnp.bfloat16

D_MODEL = 2048
CHUNK = 64
H_A = 8
DK_A = 64
DV_A = 2 * DK_A
W_QK_A = H_A * 2 * DK_A
W_A = H_A * DV_A
H_B = 8
DH_B = 128
W_B = H_B * DH_B
N_PREV = 8
BAND_ROWS = N_PREV * CHUNK
MAX_REL = 128
FFN_DIM = 4 * D_MODEL
ROPE_THETA = 10000.0
EPS = 1e-6
NEG = -1e30

HEAD_LANES = 128
BF16_SUBLANES = 16
VMEM_LIMIT = 58 * 1024 * 1024

FFN_BM = 512
FFN_BF = 1024
FFN_OWN_BM = 1024
FFN_OWN_BF = 256
PROJ_BM = 1024
PROJ_BN = 1024
MERGE_BM = 512
ATT_BQ = 256
BAND_HEADS_PER_STEP = 4
BAND_WIN = BAND_ROWS + ATT_BQ
BIAS_RULER = BAND_WIN + ATT_BQ


def _params(*sem):
    return pltpu.CompilerParams(dimension_semantics=sem, vmem_limit_bytes=VMEM_LIMIT)


def _rms(x, g):
    return x * lax.rsqrt(jnp.mean(x * x, axis=-1, keepdims=True) + EPS) * g


def _ffn_kernel(x_ref, g_ref, wg_ref, wu_ref, wd_ref, g2_ref, *rest, mode, ncast, own_f32):
    cast_in, rest = rest[:ncast], rest[ncast:]
    if mode == "mix":
        o_ref, hn_ref, *rest = rest
    else:
        o_ref, *rest = rest
    cast_out, rest = rest[:ncast], rest[ncast:]
    if own_f32:
        wg16_ref, wu16_ref, wd16_ref, *rest = rest
    (h_ref,) = rest
    f = pl.program_id(1)

    for src, dst in zip(cast_in, cast_out):
        dst[...] = src[...].astype(BF16)

    @pl.when(f == 0)
    def _():
        x = x_ref[...]
        h_ref[...] = _rms(x, g_ref[...]).astype(BF16)
        o_ref[...] = x

    def weight(w_ref, w16_ref):
        if not own_f32:
            return w_ref[...]
        w16_ref[...] = w_ref[...].astype(BF16)
        return w16_ref[...]

    h = h_ref[...]
    a = jnp.dot(h, weight(wg_ref, wg16_ref if own_f32 else None), preferred_element_type=F32)
    b = jnp.dot(h, weight(wu_ref, wu16_ref if own_f32 else None), preferred_element_type=F32)
    t = (0.5 * (a * jax.nn.sigmoid(a)) * b).astype(BF16)
    o_ref[...] += jnp.dot(t, weight(wd_ref, wd16_ref if own_f32 else None), preferred_element_type=F32)

    @pl.when(f == pl.num_programs(1) - 1)
    def _():
        y = _rms(o_ref[...], g2_ref[...])
        if mode == "mix":
            hn_ref[...] = y.astype(BF16)
        else:
            o_ref[...] = y


def _ffn(x, g, wg, wu, wd, g2, mode, casts=(), own_f32=False):
    m = x.shape[0]
    bm, bf = (FFN_OWN_BM, FFN_OWN_BF) if own_f32 else (FFN_BM, FFN_BF)
    grid = (m // bm, FFN_DIM // bf)
    nf = grid[1]
    row = pl.BlockSpec((bm, D_MODEL), lambda i, f: (i, 0), pipeline_mode=pl.Buffered(1) if own_f32 else None)
    vec = pl.BlockSpec((1, D_MODEL), lambda i, f: (0, 0))
    w_specs = [
        pl.BlockSpec((D_MODEL, bf), lambda i, f: (0, f)),
        pl.BlockSpec((D_MODEL, bf), lambda i, f: (0, f)),
        pl.BlockSpec((bf, D_MODEL), lambda i, f: (f, 0)),
    ]
    in_specs = [row, vec, *w_specs, vec]
    if mode == "mix":
        out_shape = [jax.ShapeDtypeStruct((m, D_MODEL), F32), jax.ShapeDtypeStruct((m, D_MODEL), BF16)]
        out_specs = [row, row]
    else:
        out_shape = [jax.ShapeDtypeStruct((m, D_MODEL), F32)]
        out_specs = [row]
    nsteps = grid[0] * nf
    for w in casts:
        rows, cols = w.shape
        if rows % (nsteps * BF16_SUBLANES) == 0:
            block_rows, hold = rows // nsteps, 1
        else:
            block_rows, hold = BF16_SUBLANES, nsteps * BF16_SUBLANES // rows
            assert rows * hold == nsteps * BF16_SUBLANES
        spec = pl.BlockSpec((block_rows, cols), functools.partial(lambda i, f, hold: ((i * nf + f) // hold, 0), hold=hold))
        in_specs.append(spec)
        out_specs.append(spec)
        out_shape.append(jax.ShapeDtypeStruct(w.shape, BF16))
    if own_f32:
        def once(f, i):
            return jnp.where(i == 0, f, nf)
        out_specs += [pl.BlockSpec((D_MODEL, bf), lambda i, f: (0, once(f, i))),
                      pl.BlockSpec((D_MODEL, bf), lambda i, f: (0, once(f, i))),
                      pl.BlockSpec((bf, D_MODEL), lambda i, f: (once(f, i), 0))]
        spare = bf
        out_shape += [jax.ShapeDtypeStruct((D_MODEL, FFN_DIM + spare), BF16),
                      jax.ShapeDtypeStruct((D_MODEL, FFN_DIM + spare), BF16),
                      jax.ShapeDtypeStruct((FFN_DIM + spare, D_MODEL), BF16)]
    return pl.pallas_call(
        functools.partial(_ffn_kernel, mode=mode, ncast=len(casts), own_f32=own_f32),
        grid=grid, in_specs=in_specs, out_specs=out_specs, out_shape=out_shape,
        scratch_shapes=[pltpu.VMEM((bm, D_MODEL), BF16)],
        compiler_params=_params("arbitrary" if own_f32 else "parallel", "arbitrary"),
        name="ffn_" + mode + ("_f32w" if own_f32 else ""),
    )(x, g, wg, wu, wd, g2, *casts)


def _rope_slabs(z, cos, sin_signed):
    lane = lax.broadcasted_iota(jnp.int32, cos.shape, 1)
    first_half = (lane % DK_A) < (DK_A // 2)
    outs = []
    for s in range(z.shape[1] // HEAD_LANES):
        slab = z[:, s * HEAD_LANES:(s + 1) * HEAD_LANES]
        partner = jnp.where(first_half,
                            pltpu.roll(slab, HEAD_LANES - DK_A // 2, 1),
                            pltpu.roll(slab, DK_A // 2, 1))
        outs.append(slab * cos + partner * sin_signed)
    return jnp.concatenate(outs, axis=1)


def _proj_kernel(h_ref, w_ref, *rest, epilogue, transposed):
    rope = epilogue in ("rope_q", "rope_kv")
    heads = epilogue in ("rope_kv", "kv")
    if rope:
        cos_ref, sin_ref, *rest = rest
    z = jnp.dot(h_ref[...], w_ref[...], preferred_element_type=F32)
    if rope:
        z = _rope_slabs(z, cos_ref[...], sin_ref[...])
    if epilogue == "rope_q":
        z = z * (DK_A ** -0.5)
    elif epilogue == "sigmoid":
        z = jax.nn.sigmoid(z)
    nh = z.shape[1] // HEAD_LANES
    if heads:
        o_ref, o16_ref = rest
        for h in range(nh):
            zh = z[:, h * HEAD_LANES:(h + 1) * HEAD_LANES]
            o_ref[pl.ds(h, z.shape[0], stride=nh), :] = zh
            o16_ref[h] = (zh.T if transposed else zh).astype(BF16)
    elif transposed:
        (o_ref,) = rest
        for h in range(nh):
            o_ref[h] = z[:, h * HEAD_LANES:(h + 1) * HEAD_LANES].T.astype(BF16)
    else:
        (o_ref,) = rest
        o_ref[...] = z.astype(o_ref.dtype)


def _proj(hn, w_in, col0, ncols, epilogue, rope_tabs=None, rope_period=None, transposed=False):
    m = hn.shape[0]
    bm, bn = PROJ_BM, PROJ_BN
    joff = col0 // bn
    nh = bn // HEAD_LANES
    head_major = (pl.BlockSpec((nh, HEAD_LANES, bm), lambda i, j: (0, 0, i)) if transposed
                  else pl.BlockSpec((nh, bm, HEAD_LANES), lambda i, j: (0, i, 0)))
    head_major_shape = jax.ShapeDtypeStruct((nh, HEAD_LANES, m) if transposed else (nh, m, HEAD_LANES), BF16)
    in_specs = [
        pl.BlockSpec((bm, D_MODEL), lambda i, j: (i, 0)),
        pl.BlockSpec((D_MODEL, bn), lambda i, j: (0, j + joff)),
    ]
    args = [hn, w_in]
    if rope_tabs is not None:
        nper = rope_period // bm
        tab = pl.BlockSpec((bm, HEAD_LANES), lambda i, j: (i % nper, 0))
        in_specs += [tab, tab]
        args += list(rope_tabs)
    if epilogue in ("rope_kv", "kv"):
        assert ncols == bn
        out_specs = (pl.BlockSpec((bm * nh, HEAD_LANES), lambda i, j: (i, 0)), head_major)
        out_shape = (jax.ShapeDtypeStruct((m * nh, HEAD_LANES), F32), head_major_shape)
    elif transposed:
        assert ncols == bn
        out_specs, out_shape = head_major, head_major_shape
    else:
        out_specs = pl.BlockSpec((bm, bn), lambda i, j: (i, j))
        out_shape = jax.ShapeDtypeStruct((m, ncols), BF16)
    return pl.pallas_call(
        functools.partial(_proj_kernel, epilogue=epilogue, transposed=transposed),
        grid=(m // bm, ncols // bn), in_specs=in_specs,
        out_specs=out_specs, out_shape=out_shape,
        compiler_params=_params("parallel", "arbitrary"),
        name="proj_" + epilogue,
    )(*args)


def _block_diag_q(q):
    lane = lax.broadcasted_iota(jnp.int32, q.shape, 1)
    zero = jnp.zeros_like(q)
    return jnp.concatenate([jnp.where(lane < DK_A, q, zero), jnp.where(lane >= DK_A, q, zero)], axis=0)


def _qk(q, k):
    return lax.dot_general(q, k, (((1,), (1,)), ((), ())), preferred_element_type=F32)


def _attn_a_prompt_kernel(lam_ref, qt_ref, k_ref, vt_ref, subln_ref, o_ref, *, lam_scale):
    qi = pl.program_id(1)
    nh, _, bq = qt_ref.shape
    row = lax.broadcasted_iota(jnp.int32, (HEAD_LANES, bq), 0)
    qbd = []
    for h in range(nh):
        qt = qt_ref[h]
        zero = jnp.zeros_like(qt)
        qbd.append(jnp.concatenate([jnp.where(row < DK_A, qt, zero), jnp.where(row >= DK_A, qt, zero)], axis=1))

    def step(j, carries, masked):
        start = pl.multiple_of(j * bq, bq)
        scores = [jnp.dot(k_ref[h, pl.ds(start, bq), :], qbd[h], preferred_element_type=F32) for h in range(nh)]
        if masked:
            kr = lax.broadcasted_iota(jnp.int32, (bq, 2 * bq), 0)
            qc = lax.broadcasted_iota(jnp.int32, (bq, 2 * bq), 1) % bq
            visible = (kr // CHUNK) <= (qc // CHUNK)
        out = []
        for h in range(nh):
            m, l, acc = carries[h]
            s = scores[h]
            if masked:
                s = jnp.where(visible, s, NEG)
            m_new = jnp.maximum(m, jnp.max(s, axis=0, keepdims=True))
            alpha = jnp.exp(m - m_new)
            p = jnp.exp(s - m_new)
            l = alpha * l + jnp.sum(p, axis=0, keepdims=True)
            acc = alpha * acc + jnp.dot(vt_ref[h, :, pl.ds(start, bq)], p.astype(BF16),
                                        preferred_element_type=F32)
            out.append((m_new, l, acc))
        return tuple(out)

    init = (jnp.full((1, 2 * bq), -jnp.inf, F32), jnp.zeros((1, 2 * bq), F32),
            jnp.zeros((HEAD_LANES, 2 * bq), F32))
    carries = lax.fori_loop(0, qi, functools.partial(step, masked=False), (init,) * nh)
    carries = step(qi, carries, masked=True)
    for h in range(nh):
        _, l, acc = carries[h]
        o = acc / l
        o = (o[:, :bq] - lam_ref[0] * o[:, bq:]).T
        o = _rms(o, subln_ref[...]) * lam_scale
        o_ref[:, h * HEAD_LANES:(h + 1) * HEAD_LANES] = o.astype(o_ref.dtype)


def _attn_a_prompt(lam, qt, k16, vt, subln, batch, seq, lam_scale):
    bq = ATT_BQ
    nq = seq // bq
    return pl.pallas_call(
        functools.partial(_attn_a_prompt_kernel, lam_scale=lam_scale),
        grid=(batch, nq),
        in_specs=[
            pl.BlockSpec(memory_space=pltpu.SMEM),
            pl.BlockSpec((H_A, HEAD_LANES, bq), lambda b, i: (0, 0, b * nq + i)),
            pl.BlockSpec((H_A, seq, HEAD_LANES), lambda b, i: (0, b, 0)),
            pl.BlockSpec((H_A, HEAD_LANES, seq), lambda b, i: (0, 0, b)),
            pl.BlockSpec((1, DV_A), lambda b, i: (0, 0)),
        ],
        out_specs=pl.BlockSpec((bq, W_A), lambda b, i: (b * nq + i, 0)),
        out_shape=jax.ShapeDtypeStruct((batch * seq, W_A), BF16),
        compiler_params=_params("parallel", "arbitrary"),
        name="attn_a_prompt",
    )(lam, qt, k16, vt, subln)


def _softmax_two_parts(sc, sn, vc, vn):
    m = jnp.maximum(jnp.max(sc, axis=-1, keepdims=True), jnp.max(sn, axis=-1, keepdims=True))
    pc = jnp.exp(sc - m)
    pn = jnp.exp(sn - m)
    l = jnp.sum(pc, axis=-1, keepdims=True) + jnp.sum(pn, axis=-1, keepdims=True)
    acc = (jnp.dot(pc.astype(BF16), vc, preferred_element_type=F32)
           + jnp.dot(pn.astype(BF16), vn, preferred_element_type=F32))
    return acc, l


def _tn(a, b):
    return lax.dot_general(a, b, (((0,), (0,)), ((), ())), preferred_element_type=F32)


def _attn_a_sample_kernel(lam_ref, q_ref, kc_ref, vc_ref, kn_ref, vn_ref, subln_ref, o_ref, *, lam_scale):
    n = q_ref.shape[0]
    past = kc_ref.shape[0] // H_A
    scores = []
    for h in range(H_A):
        qbd_t = _block_diag_q(q_ref[:, h * HEAD_LANES:(h + 1) * HEAD_LANES]).astype(F32).T.astype(BF16)
        kc = kc_ref[pl.ds(h, past, stride=H_A), :].astype(BF16)
        scores.append((jnp.dot(kc, qbd_t, preferred_element_type=F32),
                       jnp.dot(kn_ref[h], qbd_t, preferred_element_type=F32)))
    for h in range(H_A):
        sc, sn = scores[h]
        m = jnp.maximum(jnp.max(sc, axis=0, keepdims=True), jnp.max(sn, axis=0, keepdims=True))
        pc = jnp.exp(sc - m)
        pn = jnp.exp(sn - m)
        l = jnp.sum(pc, axis=0, keepdims=True) + jnp.sum(pn, axis=0, keepdims=True)
        acc = (_tn(vc_ref[pl.ds(h, past, stride=H_A), :].astype(BF16), pc.astype(BF16))
               + _tn(vn_ref[h], pn.astype(BF16)))
        o = acc / l
        o = (o[:, :n] - lam_ref[0] * o[:, n:]).T
        o = _rms(o, subln_ref[...]) * lam_scale
        o_ref[:, h * HEAD_LANES:(h + 1) * HEAD_LANES] = o.astype(o_ref.dtype)


def _attn_a_sample(lam, qa, cache_k, cache_v, k16, v16, subln, layer, lam_scale):
    _, nb, cache_rows, _ = cache_k.shape
    n = qa.shape[0] // nb
    rows = pl.BlockSpec((n, W_A), lambda b: (b, 0))
    new = pl.BlockSpec((H_A, n, HEAD_LANES), lambda b: (0, b, 0))
    old = pl.BlockSpec((None, None, cache_rows, HEAD_LANES), lambda b: (layer, b, 0, 0))
    return pl.pallas_call(
        functools.partial(_attn_a_sample_kernel, lam_scale=lam_scale),
        grid=(nb,),
        in_specs=[pl.BlockSpec(memory_space=pltpu.SMEM), rows, old, old, new, new,
                  pl.BlockSpec((1, DV_A), lambda b: (0, 0))],
        out_specs=rows,
        out_shape=jax.ShapeDtypeStruct(qa.shape, BF16),
        compiler_params=_params("parallel"),
        name="attn_a_sample",
    )(lam, qa, cache_k, cache_v, k16, v16, subln)


def _toeplitz_bias(ruler_row, nq, nk):
    rows = jnp.broadcast_to(ruler_row, (nq, ruler_row.shape[1]))
    return pltpu.roll(rows, 0, 1, stride=1, stride_axis=0)[:, :nk]


def _attn_b_prompt_kernel(q_ref, k_ref, v_ref, ruler_ref, o_ref, kb_ref, vb_ref, bias_ref):
    qi = pl.program_id(2)
    bq = q_ref.shape[0]
    ng, seq, _ = k_ref.shape

    @pl.when(qi == 0)
    def _():
        zeros = jnp.zeros((BAND_ROWS, HEAD_LANES), BF16)
        d = (lax.broadcasted_iota(jnp.int32, (bq, BAND_WIN), 1) // CHUNK
             - lax.broadcasted_iota(jnp.int32, (bq, BAND_WIN), 0) // CHUNK)
        for g in range(ng):
            kb_ref[g, 0:BAND_ROWS, :] = zeros
            vb_ref[g, 0:BAND_ROWS, :] = zeros
            kb_ref[g, BAND_ROWS:BAND_ROWS + seq, :] = k_ref[g]
            vb_ref[g, BAND_ROWS:BAND_ROWS + seq, :] = v_ref[g]
            bias = _toeplitz_bias(ruler_ref[g], bq, BAND_WIN)
            bias_ref[g] = jnp.where(d >= 0, jnp.where(d <= N_PREV, bias, NEG), NEG)

    start = pl.multiple_of(qi * bq, bq)
    win = pl.ds(start, BAND_WIN)
    scores = [_qk(q_ref[:, g * HEAD_LANES:(g + 1) * HEAD_LANES], kb_ref[g, win, :]) for g in range(ng)]
    w = lax.broadcasted_iota(jnp.int32, (bq, BAND_WIN), 1)
    in_sequence = w + qi * bq >= BAND_ROWS
    for g in range(ng):
        s = scores[g] * (DH_B ** -0.5) + bias_ref[g]
        s = jnp.where(in_sequence, s, NEG)
        p = jnp.exp(s - jnp.max(s, axis=-1, keepdims=True))
        l = jnp.sum(p, axis=-1, keepdims=True)
        o = jnp.dot(p.astype(BF16), vb_ref[g, win, :], preferred_element_type=F32) / l
        o_ref[:, g * HEAD_LANES:(g + 1) * HEAD_LANES] = o.astype(o_ref.dtype)


def _attn_b_prompt(qb, k16, v16, ruler, batch, seq):
    bq, ng = ATT_BQ, BAND_HEADS_PER_STEP
    nq = seq // bq
    kv = pl.BlockSpec((ng, seq, HEAD_LANES), lambda b, h, i: (h, b, 0))
    rows = pl.BlockSpec((bq, ng * HEAD_LANES), lambda b, h, i: (b * nq + i, h))
    return pl.pallas_call(
        _attn_b_prompt_kernel,
        grid=(batch, H_B // ng, nq),
        in_specs=[rows, kv, kv, pl.BlockSpec((ng, 1, BIAS_RULER), lambda b, h, i: (h, 0, 0))],
        out_specs=rows,
        out_shape=jax.ShapeDtypeStruct((batch * seq, W_B), BF16),
        scratch_shapes=[pltpu.VMEM((ng, BAND_ROWS + seq, HEAD_LANES), BF16),
                        pltpu.VMEM((ng, BAND_ROWS + seq, HEAD_LANES), BF16),
                        pltpu.VMEM((ng, bq, BAND_WIN), F32)],
        compiler_params=_params("parallel", "parallel", "arbitrary"),
        name="attn_b_prompt",
    )(qb, k16, v16, ruler)


def _attn_b_sample_kernel(q_ref, kc_ref, vc_ref, kn_ref, vn_ref, ruler_ref, o_ref, bias_ref):
    n = q_ref.shape[0]
    past = kc_ref.shape[0] // H_B
    scale = DH_B ** -0.5

    @pl.when(pl.program_id(0) == 0)
    def _():
        for h in range(H_B):
            bias_ref[h] = _toeplitz_bias(ruler_ref[h], n, past + n)

    scores = []
    for h in range(H_B):
        q = q_ref[:, h * HEAD_LANES:(h + 1) * HEAD_LANES]
        scores.append((_qk(q, kc_ref[pl.ds(h, past, stride=H_B), :].astype(BF16)), _qk(q, kn_ref[h])))
    for h in range(H_B):
        sc = scores[h][0] * scale + bias_ref[h, :, :past]
        sn = scores[h][1] * scale + bias_ref[h, :, past:]
        acc, l = _softmax_two_parts(sc, sn, vc_ref[pl.ds(h, past, stride=H_B), :].astype(BF16), vn_ref[h])
        o_ref[:, h * HEAD_LANES:(h + 1) * HEAD_LANES] = (acc / l).astype(o_ref.dtype)


def _attn_b_sample(qb, cache_k, cache_v, k16, v16, ruler, layer):
    _, nb, cache_rows, _ = cache_k.shape
    past = cache_rows // H_B
    n = qb.shape[0] // nb
    rows = pl.BlockSpec((n, W_B), lambda b: (b, 0))
    new = pl.BlockSpec((H_B, n, HEAD_LANES), lambda b: (0, b, 0))
    old = pl.BlockSpec((None, None, cache_rows, HEAD_LANES), lambda b: (layer, b, 0, 0))
    return pl.pallas_call(
        _attn_b_sample_kernel,
        grid=(nb,),
        in_specs=[rows, old, old, new, new,
                  pl.BlockSpec((H_B, 1, BIAS_RULER), lambda b: (0, 0, 0))],
        out_specs=rows,
        out_shape=jax.ShapeDtypeStruct(qb.shape, BF16),
        scratch_shapes=[pltpu.VMEM((H_B, n, past + n), F32)],
        compiler_params=_params("arbitrary"),
        name="attn_b_sample",
    )(qb, cache_k, cache_v, k16, v16, ruler)


def _merge_kernel(x_ref, oa_ref, ob_ref, sga_ref, sgb_ref, wa_ref, wb_ref, wo_ref, o_ref):
    a = jnp.dot(oa_ref[...], wa_ref[...], preferred_element_type=F32)
    b = jnp.dot(ob_ref[...], wb_ref[...], preferred_element_type=F32)
    merged = sga_ref[...].astype(F32) * a + sgb_ref[...].astype(F32) * b
    o_ref[...] = x_ref[...] + jnp.dot(merged.astype(BF16), wo_ref[...], preferred_element_type=F32)


def _merge(x, oa, ob, sga, sgb, wa, wb, wo):
    m = x.shape[0]
    bm = MERGE_BM
    wide = pl.BlockSpec((bm, D_MODEL), lambda i: (i, 0))
    half = pl.BlockSpec((bm, W_A), lambda i: (i, 0))

    def whole(shape):
        return pl.BlockSpec(shape, lambda i: (0, 0), pipeline_mode=pl.Buffered(1))

    return pl.pallas_call(
        _merge_kernel,
        grid=(m // bm,),
        in_specs=[wide, half, half, wide, wide,
                  whole((W_A, D_MODEL)), whole((W_B, D_MODEL)), whole((D_MODEL, D_MODEL))],
        out_specs=wide,
        out_shape=jax.ShapeDtypeStruct((m, D_MODEL), F32),
        compiler_params=_params("parallel"),
        name="merge",
    )(x, oa, ob, sga, sgb, wa, wb, wo)


def _rope_tables(pos):
    half = DK_A // 2
    inv = ROPE_THETA ** (-jnp.arange(half, dtype=F32) / half)
    ang = pos.astype(F32)[:, None] * inv[None, :]
    cos, sin = jnp.cos(ang), jnp.sin(ang)
    reps = HEAD_LANES // DK_A
    cos_t = jnp.tile(jnp.concatenate([cos, cos], axis=1), (1, reps))
    sin_t = jnp.tile(jnp.concatenate([-sin, sin], axis=1), (1, reps))
    return cos_t, sin_t


def _bias_ruler(table):
    nh = table.shape[0]
    far_past = jnp.broadcast_to(table[:, 2 * MAX_REL:], (nh, BAND_ROWS - MAX_REL))
    far_future = jnp.broadcast_to(table[:, :1], (nh, BIAS_RULER - ATT_BQ - BAND_ROWS - MAX_REL - 1))
    wrapped = jnp.broadcast_to(table[:, 2 * MAX_REL:], (nh, ATT_BQ))
    ruler = jnp.concatenate([far_past, table[:, ::-1], far_future, wrapped], axis=1)
    return ruler.astype(F32).reshape(nh, 1, BIAS_RULER)


def kernel(x_prompt, x_sample, cache_a_k, cache_a_v, cache_b_k, cache_b_v, ffn1_norm, ffn1_w_gate, ffn1_w_up, ffn1_w_down, mix_norm, w_in, lambda_q1, lambda_k1, lambda_q2, lambda_k2, subln_a, rel_bias_b, w_branch_a, w_branch_b, w_out, ffn2_norm, ffn2_w_gate, ffn2_w_up, ffn2_w_down, final_norm):
    batch, seq, _ = x_prompt.shape
    dec_batch, dec_seq, _ = x_sample.shape
    depth = ffn1_norm.shape[0]
    past_a = cache_a_k.shape[2]
    past_b = cache_b_k.shape[2]
    assert seq % max(ATT_BQ, PROJ_BM) == 0 and seq >= BAND_ROWS
    assert dec_seq == CHUNK and past_b == BAND_ROWS and PROJ_BM % dec_seq == 0

    xp = x_prompt.reshape(batch * seq, D_MODEL)
    xs = x_sample.reshape(dec_batch * dec_seq, D_MODEL)
    rope_p = _rope_tables(jnp.arange(seq))
    rope_s = _rope_tables(past_a + (jnp.arange(PROJ_BM) % dec_seq))
    cache_a_k2 = cache_a_k.reshape(depth, dec_batch, past_a * H_A, HEAD_LANES)
    cache_a_v2 = cache_a_v.reshape(depth, dec_batch, past_a * H_A, HEAD_LANES)
    cache_b_k2 = cache_b_k.reshape(depth, dec_batch, past_b * H_B, HEAD_LANES)
    cache_b_v2 = cache_b_v.reshape(depth, dec_batch, past_b * H_B, HEAD_LANES)

    outs = {k: [] for k in ("akp", "avp", "bkp", "bvp", "aks", "avs", "bks", "bvs")}
    for layer in range(depth):
        lam_init = 0.8 - 0.6 * math.exp(-0.3 * layer)
        lam = (jnp.exp(jnp.sum(lambda_q1[layer].astype(F32) * lambda_k1[layer].astype(F32)))
               - jnp.exp(jnp.sum(lambda_q2[layer].astype(F32) * lambda_k2[layer].astype(F32)))
               + lam_init).reshape(1)
        lam_scale = 1.0 - lam_init
        g1 = ffn1_norm[layer].reshape(1, D_MODEL)
        gm = mix_norm[layer].reshape(1, D_MODEL)
        g2 = ffn2_norm[layer].reshape(1, D_MODEL)
        gf = final_norm.reshape(1, D_MODEL)
        subln = subln_a[layer].reshape(1, DV_A)
        table = rel_bias_b[layer]
        ruler = _bias_ruler(table)
        last = layer == depth - 1

        x1_s, hn_s, w1g, w1u, w1d = _ffn(xs, g1, ffn1_w_gate[layer], ffn1_w_up[layer], ffn1_w_down[layer],
                                         gm, "mix", own_f32=True)
        later = [w[layer] for w in (ffn2_w_gate, ffn2_w_up, ffn2_w_down, w_in, w_branch_a, w_branch_b, w_out)]
        x1_p, hn_p, w2g, w2u, w2d, win, wa, wb, wo = _ffn(xp, g1, w1g, w1u, w1d, gm, "mix", casts=later)

        def trunk(x1, hn, rope_tabs, rope_period, attend, transposed_a):
            c = 0
            qa = _proj(hn, win, c, W_QK_A, "rope_q", rope_tabs, rope_period, transposed=transposed_a); c += W_QK_A
            ka, ka16 = _proj(hn, win, c, W_QK_A, "rope_kv", rope_tabs, rope_period); c += W_QK_A
            va, va16 = _proj(hn, win, c, W_A, "kv", transposed=transposed_a); c += W_A
            qb = _proj(hn, win, c, W_B, "plain"); c += W_B
            kb, kb16 = _proj(hn, win, c, W_B, "kv"); c += W_B
            vb, vb16 = _proj(hn, win, c, W_B, "kv"); c += W_B
            sga = _proj(hn, win, c, D_MODEL, "sigmoid"); c += D_MODEL
            sgb = _proj(hn, win, c, D_MODEL, "sigmoid")
            oa, ob = attend(qa, ka16, va16, qb, kb16, vb16)
            x2 = _merge(x1, oa, ob, sga, sgb, wa, wb, wo)
            xo = _ffn(x2, g2, w2g, w2u, w2d, gf, "final" if last else "mix")[0]
            return xo, ka, va, kb, vb

        def attend_prompt(qa, ka16, va16, qb, kb16, vb16):
            oa = _attn_a_prompt(lam, qa, ka16, va16, subln, batch, seq, lam_scale)
            ob = _attn_b_prompt(qb, kb16, vb16, ruler, batch, seq)
            return oa, ob

        def attend_sample(qa, ka16, va16, qb, kb16, vb16):
            oa = _attn_a_sample(lam, qa, cache_a_k2, cache_a_v2, ka16, va16, subln, layer, lam_scale)
            ob = _attn_b_sample(qb, cache_b_k2, cache_b_v2, kb16, vb16, ruler, layer)
            return oa, ob

        xp, ka, va, kb, vb = trunk(x1_p, hn_p, rope_p, seq, attend_prompt, True)
        outs["akp"].append(ka.reshape(batch, seq, H_A, 2 * DK_A))
        outs["avp"].append(va.reshape(batch, seq, H_A, DV_A))
        rows = min(BAND_ROWS, seq)
        outs["bkp"].append(kb.reshape(batch, seq, H_B, DH_B)[:, seq - rows:])
        outs["bvp"].append(vb.reshape(batch, seq, H_B, DH_B)[:, seq - rows:])

        xs, ka, va, kb, vb = trunk(x1_s, hn_s, rope_s, PROJ_BM, attend_sample, False)
        outs["aks"].append(ka.reshape(dec_batch, dec_seq, H_A, 2 * DK_A))
        outs["avs"].append(va.reshape(dec_batch, dec_seq, H_A, DV_A))
        kb_all = jnp.concatenate([cache_b_k[layer], kb.reshape(dec_batch, dec_seq, H_B, DH_B)], axis=1)
        vb_all = jnp.concatenate([cache_b_v[layer], vb.reshape(dec_batch, dec_seq, H_B, DH_B)], axis=1)
        outs["bks"].append(kb_all[:, dec_seq:])
        outs["bvs"].append(vb_all[:, dec_seq:])

    y_prompt = xp.reshape(batch, seq, D_MODEL)
    y_sample = xs.reshape(dec_batch, dec_seq, D_MODEL)
    return (y_prompt, y_sample,
            jnp.stack(outs["akp"]), jnp.stack(outs["avp"]), jnp.stack(outs["bkp"]), jnp.stack(outs["bvp"]),
            jnp.stack(outs["aks"]), jnp.stack(outs["avs"]), jnp.stack(outs["bks"]), jnp.stack(outs["bvs"]))
```

```python
import functools
import math

import jax
import jax.numpy as jnp
from jax import lax
from jax.experimental import pallas as pl
from jax.experimental.pallas import tpu as pltpu

F32 = jnp.float32
BF16 = jnp.bfloat16

D_MODEL = 2048
CHUNK = 64
H_A = 8
DK_A = 64
DV_A = 2 * DK_A
W_QK_A = H_A * 2 * DK_A
W_A = H_A * DV_A
H_B = 8
DH_B = 128
W_B = H_B * DH_B
N_PREV = 8
BAND_ROWS = N_PREV * CHUNK
MAX_REL = 128
FFN_DIM = 4 * D_MODEL
ROPE_THETA = 10000.0
EPS = 1e-6
NEG = -1e30

HEAD_LANES = 128
BF16_SUBLANES = 16
VMEM_LIMIT = 58 * 1024 * 1024

FFN_BM = 512
FFN_BF = 1024
FFN_OWN_BM = 1024
FFN_OWN_BF = 256
PROJ_BM = 1024
PROJ_BN = 1024
MERGE_BM = 512
ATT_BQ = 256
BAND_HEADS_PER_STEP = 4
BAND_WIN = BAND_ROWS + ATT_BQ
BIAS_RULER = BAND_WIN + ATT_BQ


def _params(*sem):
    return pltpu.CompilerParams(dimension_semantics=sem, vmem_limit_bytes=VMEM_LIMIT)


def _rms(x, g):
    return x * lax.rsqrt(jnp.mean(x * x, axis=-1, keepdims=True) + EPS) * g


def _ffn_kernel(x_ref, g_ref, wg_ref, wu_ref, wd_ref, g2_ref, *rest, mode, ncast, own_f32):
    cast_in, rest = rest[:ncast], rest[ncast:]
    if mode == "mix":
        o_ref, hn_ref, *rest = rest
    else:
        o_ref, *rest = rest
    cast_out, rest = rest[:ncast], rest[ncast:]
    if own_f32:
        wg16_ref, wu16_ref, wd16_ref, *rest = rest
    (h_ref,) = rest
    f = pl.program_id(1)

    for src, dst in zip(cast_in, cast_out):
        dst[...] = src[...].astype(BF16)

    @pl.when(f == 0)
    def _():
        x = x_ref[...]
        h_ref[...] = _rms(x, g_ref[...]).astype(BF16)
        o_ref[...] = x

    def weight(w_ref, w16_ref):
        if not own_f32:
            return w_ref[...]
        w16_ref[...] = w_ref[...].astype(BF16)
        return w16_ref[...]

    h = h_ref[...]
    a = jnp.dot(h, weight(wg_ref, wg16_ref if own_f32 else None), preferred_element_type=F32)
    b = jnp.dot(h, weight(wu_ref, wu16_ref if own_f32 else None), preferred_element_type=F32)
    t = (0.5 * (a * jax.nn.sigmoid(a)) * b).astype(BF16)
    o_ref[...] += jnp.dot(t, weight(wd_ref, wd16_ref if own_f32 else None), preferred_element_type=F32)

    @pl.when(f == pl.num_programs(1) - 1)
    def _():
        y = _rms(o_ref[...], g2_ref[...])
        if mode == "mix":
            hn_ref[...] = y.astype(BF16)
        else:
            o_ref[...] = y


def _ffn(x, g, wg, wu, wd, g2, mode, casts=(), own_f32=False):
    m = x.shape[0]
    bm, bf = (FFN_OWN_BM, FFN_OWN_BF) if own_f32 else (FFN_BM, FFN_BF)
    grid = (m // bm, FFN_DIM // bf)
    nf = grid[1]
    row = pl.BlockSpec((bm, D_MODEL), lambda i, f: (i, 0), pipeline_mode=pl.Buffered(1) if own_f32 else None)
    vec = pl.BlockSpec((1, D_MODEL), lambda i, f: (0, 0))
    w_specs = [
        pl.BlockSpec((D_MODEL, bf), lambda i, f: (0, f)),
        pl.BlockSpec((D_MODEL, bf), lambda i, f: (0, f)),
        pl.BlockSpec((bf, D_MODEL), lambda i, f: (f, 0)),
    ]
    in_specs = [row, vec, *w_specs, vec]
    if mode == "mix":
        out_shape = [jax.ShapeDtypeStruct((m, D_MODEL), F32), jax.ShapeDtypeStruct((m, D_MODEL), BF16)]
        out_specs = [row, row]
    else:
        out_shape = [jax.ShapeDtypeStruct((m, D_MODEL), F32)]
        out_specs = [row]
    nsteps = grid[0] * nf
    for w in casts:
        rows, cols = w.shape
        if rows % (nsteps * BF16_SUBLANES) == 0:
            block_rows, hold = rows // nsteps, 1
        else:
            block_rows, hold = BF16_SUBLANES, nsteps * BF16_SUBLANES // rows
            assert rows * hold == nsteps * BF16_SUBLANES
        spec = pl.BlockSpec((block_rows, cols), functools.partial(lambda i, f, hold: ((i * nf + f) // hold, 0), hold=hold))
        in_specs.append(spec)
        out_specs.append(spec)
        out_shape.append(jax.ShapeDtypeStruct(w.shape, BF16))
    if own_f32:
        def once(f, i):
            return jnp.where(i == 0, f, nf)
        out_specs += [pl.BlockSpec((D_MODEL, bf), lambda i, f: (0, once(f, i))),
                      pl.BlockSpec((D_MODEL, bf), lambda i, f: (0, once(f, i))),
                      pl.BlockSpec((bf, D_MODEL), lambda i, f: (once(f, i), 0))]
        spare = bf
        out_shape += [jax.ShapeDtypeStruct((D_MODEL, FFN_DIM + spare), BF16),
                      jax.ShapeDtypeStruct((D_MODEL, FFN_DIM + spare), BF16),
                      jax.ShapeDtypeStruct((FFN_DIM + spare, D_MODEL), BF16)]
    return pl.pallas_call(
        functools.partial(_ffn_kernel, mode=mode, ncast=len(casts), own_f32=own_f32),
        grid=grid, in_specs=in_specs, out_specs=out_specs, out_shape=out_shape,
        scratch_shapes=[pltpu.VMEM((bm, D_MODEL), BF16)],
        compiler_params=_params("arbitrary" if own_f32 else "parallel", "arbitrary"),
        name="ffn_" + mode + ("_f32w" if own_f32 else ""),
    )(x, g, wg, wu, wd, g2, *casts)


def _rope_slabs(z, cos, sin_signed):
    lane = lax.broadcasted_iota(jnp.int32, cos.shape, 1)
    first_half = (lane % DK_A) < (DK_A // 2)
    outs = []
    for s in range(z.shape[1] // HEAD_LANES):
        slab = z[:, s * HEAD_LANES:(s + 1) * HEAD_LANES]
        partner = jnp.where(first_half,
                            pltpu.roll(slab, HEAD_LANES - DK_A // 2, 1),
                            pltpu.roll(slab, DK_A // 2, 1))
        outs.append(slab * cos + partner * sin_signed)
    return jnp.concatenate(outs, axis=1)


def _proj_kernel(h_ref, w_ref, *rest, epilogue, transposed):
    rope = epilogue in ("rope_q", "rope_kv")
    heads = epilogue in ("rope_kv", "kv")
    if rope:
        cos_ref, sin_ref, *rest = rest
    z = jnp.dot(h_ref[...], w_ref[...], preferred_element_type=F32)
    if rope:
        z = _rope_slabs(z, cos_ref[...], sin_ref[...])
    if epilogue == "rope_q":
        z = z * (DK_A ** -0.5)
    elif epilogue == "sigmoid":
        z = jax.nn.sigmoid(z)
    nh = z.shape[1] // HEAD_LANES
    if heads:
        o_ref, o16_ref = rest
        for h in range(nh):
            zh = z[:, h * HEAD_LANES:(h + 1) * HEAD_LANES]
            o_ref[pl.ds(h, z.shape[0], stride=nh), :] = zh
            o16_ref[h] = (zh.T if transposed else zh).astype(BF16)
    elif transposed:
        (o_ref,) = rest
        for h in range(nh):
            o_ref[h] = z[:, h * HEAD_LANES:(h + 1) * HEAD_LANES].T.astype(BF16)
    else:
        (o_ref,) = rest
        o_ref[...] = z.astype(o_ref.dtype)


def _proj(hn, w_in, col0, ncols, epilogue, rope_tabs=None, rope_period=None, transposed=False):
    m = hn.shape[0]
    bm, bn = PROJ_BM, PROJ_BN
    joff = col0 // bn
    nh = bn // HEAD_LANES
    head_major = (pl.BlockSpec((nh, HEAD_LANES, bm), lambda i, j: (0, 0, i)) if transposed
                  else pl.BlockSpec((nh, bm, HEAD_LANES), lambda i, j: (0, i, 0)))
    head_major_shape = jax.ShapeDtypeStruct((nh, HEAD_LANES, m) if transposed else (nh, m, HEAD_LANES), BF16)
    in_specs = [
        pl.BlockSpec((bm, D_MODEL), lambda i, j: (i, 0)),
        pl.BlockSpec((D_MODEL, bn), lambda i, j: (0, j + joff)),
    ]
    args = [hn, w_in]
    if rope_tabs is not None:
        nper = rope_period // bm
        tab = pl.BlockSpec((bm, HEAD_LANES), lambda i, j: (i % nper, 0))
        in_specs += [tab, tab]
        args += list(rope_tabs)
    if epilogue in ("rope_kv", "kv"):
        assert ncols == bn
        out_specs = (pl.BlockSpec((bm * nh, HEAD_LANES), lambda i, j: (i, 0)), head_major)
        out_shape = (jax.ShapeDtypeStruct((m * nh, HEAD_LANES), F32), head_major_shape)
    elif transposed:
        assert ncols == bn
        out_specs, out_shape = head_major, head_major_shape
    else:
        out_specs = pl.BlockSpec((bm, bn), lambda i, j: (i, j))
        out_shape = jax.ShapeDtypeStruct((m, ncols), BF16)
    return pl.pallas_call(
        functools.partial(_proj_kernel, epilogue=epilogue, transposed=transposed),
        grid=(m // bm, ncols // bn), in_specs=in_specs,
        out_specs=out_specs, out_shape=out_shape,
        compiler_params=_params("parallel", "arbitrary"),
        name="proj_" + epilogue,
    )(*args)


def _block_diag_q(q):
    lane = lax.broadcasted_iota(jnp.int32, q.shape, 1)
    zero = jnp.zeros_like(q)
    return jnp.concatenate([jnp.where(lane < DK_A, q, zero), jnp.where(lane >= DK_A, q, zero)], axis=0)


def _qk(q, k):
    return lax.dot_general(q, k, (((1,), (1,)), ((), ())), preferred_element_type=F32)


def _attn_a_prompt_kernel(lam_ref, qt_ref, k_ref, vt_ref, subln_ref, o_ref, *, lam_scale):
    qi = pl.program_id(1)
    nh, _, bq = qt_ref.shape
    row = lax.broadcasted_iota(jnp.int32, (HEAD_LANES, bq), 0)
    qbd = []
    for h in range(nh):
        qt = qt_ref[h]
        zero = jnp.zeros_like(qt)
        qbd.append(jnp.concatenate([jnp.where(row < DK_A, qt, zero), jnp.where(row >= DK_A, qt, zero)], axis=1))

    def step(start, bk, carries, masked):
        scores = [jnp.dot(k_ref[h, pl.ds(start, bk), :], qbd[h], preferred_element_type=F32) for h in range(nh)]
        if masked:
            kr = lax.broadcasted_iota(jnp.int32, (bq, 2 * bq), 0)
            qc = lax.broadcasted_iota(jnp.int32, (bq, 2 * bq), 1) % bq
            visible = (kr // CHUNK) <= (qc // CHUNK)
        out = []
        for h in range(nh):
            m, l, acc = carries[h]
            s = scores[h]
            if masked:
                s = jnp.where(visible, s, NEG)
            m_new = jnp.maximum(m, jnp.max(s, axis=0, keepdims=True))
            alpha = jnp.exp(m - m_new)
            p = jnp.exp(s - m_new)
            l = alpha * l + jnp.sum(p, axis=0, keepdims=True)
            acc = alpha * acc + jnp.dot(vt_ref[h, :, pl.ds(start, bk)], p.astype(BF16),
                                        preferred_element_type=F32)
            out.append((m_new, l, acc))
        return tuple(out)

    init = (jnp.full((1, 2 * bq), -jnp.inf, F32), jnp.zeros((1, 2 * bq), F32),
            jnp.zeros((HEAD_LANES, 2 * bq), F32))
    wide = 2 * bq
    carries = lax.fori_loop(
        0, qi // 2, lambda j, c: step(pl.multiple_of(j * wide, wide), wide, c, masked=False), (init,) * nh)
    carries = lax.fori_loop(
        0, qi % 2, lambda _, c: step(pl.multiple_of((qi - 1) * bq, bq), bq, c, masked=False), carries)
    carries = step(pl.multiple_of(qi * bq, bq), bq, carries, masked=True)
    for h in range(nh):
        _, l, acc = carries[h]
        o = acc / l
        o = (o[:, :bq] - lam_ref[0] * o[:, bq:]).T
        o = _rms(o, subln_ref[...]) * lam_scale
        o_ref[:, h * HEAD_LANES:(h + 1) * HEAD_LANES] = o.astype(o_ref.dtype)


def _attn_a_prompt(lam, qt, k16, vt, subln, batch, seq, lam_scale):
    bq = ATT_BQ
    nq = seq // bq
    return pl.pallas_call(
        functools.partial(_attn_a_prompt_kernel, lam_scale=lam_scale),
        grid=(batch, nq),
        in_specs=[
            pl.BlockSpec(memory_space=pltpu.SMEM),
            pl.BlockSpec((H_A, HEAD_LANES, bq), lambda b, i: (0, 0, b * nq + i)),
            pl.BlockSpec((H_A, seq, HEAD_LANES), lambda b, i: (0, b, 0)),
            pl.BlockSpec((H_A, HEAD_LANES, seq), lambda b, i: (0, 0, b)),
            pl.BlockSpec((1, DV_A), lambda b, i: (0, 0)),
        ],
        out_specs=pl.BlockSpec((bq, W_A), lambda b, i: (b * nq + i, 0)),
        out_shape=jax.ShapeDtypeStruct((batch * seq, W_A), BF16),
        compiler_params=_params("parallel", "arbitrary"),
        name="attn_a_prompt",
    )(lam, qt, k16, vt, subln)


def _softmax_two_parts(sc, sn, vc, vn):
    m = jnp.maximum(jnp.max(sc, axis=-1, keepdims=True), jnp.max(sn, axis=-1, keepdims=True))
    pc = jnp.exp(sc - m)
    pn = jnp.exp(sn - m)
    l = jnp.sum(pc, axis=-1, keepdims=True) + jnp.sum(pn, axis=-1, keepdims=True)
    acc = (jnp.dot(pc.astype(BF16), vc, preferred_element_type=F32)
           + jnp.dot(pn.astype(BF16), vn, preferred_element_type=F32))
    return acc, l


def _tn(a, b):
    return lax.dot_general(a, b, (((0,), (0,)), ((), ())), preferred_element_type=F32)


def _attn_a_sample_kernel(lam_ref, q_ref, kc_ref, vc_ref, kn_ref, vn_ref, subln_ref, o_ref, *, lam_scale):
    n = q_ref.shape[0]
    past = kc_ref.shape[0] // H_A
    scores = []
    for h in range(H_A):
        qbd_t = _block_diag_q(q_ref[:, h * HEAD_LANES:(h + 1) * HEAD_LANES]).astype(F32).T.astype(BF16)
        kc = kc_ref[pl.ds(h, past, stride=H_A), :].astype(BF16)
        scores.append((jnp.dot(kc, qbd_t, preferred_element_type=F32),
                       jnp.dot(kn_ref[h], qbd_t, preferred_element_type=F32)))
    for h in range(H_A):
        sc, sn = scores[h]
        m = jnp.maximum(jnp.max(sc, axis=0, keepdims=True), jnp.max(sn, axis=0, keepdims=True))
        pc = jnp.exp(sc - m)
        pn = jnp.exp(sn - m)
        l = jnp.sum(pc, axis=0, keepdims=True) + jnp.sum(pn, axis=0, keepdims=True)
        acc = (_tn(vc_ref[pl.ds(h, past, stride=H_A), :].astype(BF16), pc.astype(BF16))
               + _tn(vn_ref[h], pn.astype(BF16)))
        o = acc / l
        o = (o[:, :n] - lam_ref[0] * o[:, n:]).T
        o = _rms(o, subln_ref[...]) * lam_scale
        o_ref[:, h * HEAD_LANES:(h + 1) * HEAD_LANES] = o.astype(o_ref.dtype)


def _attn_a_sample(lam, qa, cache_k, cache_v, k16, v16, subln, layer, lam_scale):
    _, nb, cache_rows, _ = cache_k.shape
    n = qa.shape[0] // nb
    rows = pl.BlockSpec((n, W_A), lambda b: (b, 0))
    new = pl.BlockSpec((H_A, n, HEAD_LANES), lambda b: (0, b, 0))
    old = pl.BlockSpec((None, None, cache_rows, HEAD_LANES), lambda b: (layer, b, 0, 0))
    return pl.pallas_call(
        functools.partial(_attn_a_sample_kernel, lam_scale=lam_scale),
        grid=(nb,),
        in_specs=[pl.BlockSpec(memory_space=pltpu.SMEM), rows, old, old, new, new,
                  pl.BlockSpec((1, DV_A), lambda b: (0, 0))],
        out_specs=rows,
        out_shape=jax.ShapeDtypeStruct(qa.shape, BF16),
        compiler_params=_params("parallel"),
        name="attn_a_sample",
    )(lam, qa, cache_k, cache_v, k16, v16, subln)


def _toeplitz_bias(ruler_row, nq, nk):
    rows = jnp.broadcast_to(ruler_row, (nq, ruler_row.shape[1]))
    return pltpu.roll(rows, 0, 1, stride=1, stride_axis=0)[:, :nk]


def _attn_b_prompt_kernel(q_ref, k_ref, v_ref, ruler_ref, o_ref, kb_ref, vb_ref, bias_ref):
    qi = pl.program_id(2)
    bq = q_ref.shape[0]
    ng, seq, _ = k_ref.shape

    @pl.when(qi == 0)
    def _():
        zeros = jnp.zeros((BAND_ROWS, HEAD_LANES), BF16)
        d = (lax.broadcasted_iota(jnp.int32, (bq, BAND_WIN), 1) // CHUNK
             - lax.broadcasted_iota(jnp.int32, (bq, BAND_WIN), 0) // CHUNK)
        for g in range(ng):
            kb_ref[g, 0:BAND_ROWS, :] = zeros
            vb_ref[g, 0:BAND_ROWS, :] = zeros
            kb_ref[g, BAND_ROWS:BAND_ROWS + seq, :] = k_ref[g]
            vb_ref[g, BAND_ROWS:BAND_ROWS + seq, :] = v_ref[g]
            bias = _toeplitz_bias(ruler_ref[g], bq, BAND_WIN)
            bias_ref[g] = jnp.where(d >= 0, jnp.where(d <= N_PREV, bias, NEG), NEG)

    start = pl.multiple_of(qi * bq, bq)
    win = pl.ds(start, BAND_WIN)
    scores = [_qk(q_ref[:, g * HEAD_LANES:(g + 1) * HEAD_LANES], kb_ref[g, win, :]) for g in range(ng)]
    w = lax.broadcasted_iota(jnp.int32, (bq, BAND_WIN), 1)
    in_sequence = w + qi * bq >= BAND_ROWS
    for g in range(ng):
        s = scores[g] * (DH_B ** -0.5) + bias_ref[g]
        s = jnp.where(in_sequence, s, NEG)
        p = jnp.exp(s - jnp.max(s, axis=-1, keepdims=True))
        l = jnp.sum(p, axis=-1, keepdims=True)
        o = jnp.dot(p.astype(BF16), vb_ref[g, win, :], preferred_element_type=F32) / l
        o_ref[:, g * HEAD_LANES:(g + 1) * HEAD_LANES] = o.astype(o_ref.dtype)


def _attn_b_prompt(qb, k16, v16, ruler, batch, seq):
    bq, ng = ATT_BQ, BAND_HEADS_PER_STEP
    nq = seq // bq
    kv = pl.BlockSpec((ng, seq, HEAD_LANES), lambda b, h, i: (h, b, 0))
    rows = pl.BlockSpec((bq, ng * HEAD_LANES), lambda b, h, i: (b * nq + i, h))
    return pl.pallas_call(
        _attn_b_prompt_kernel,
        grid=(batch, H_B // ng, nq),
        in_specs=[rows, kv, kv, pl.BlockSpec((ng, 1, BIAS_RULER), lambda b, h, i: (h, 0, 0))],
        out_specs=rows,
        out_shape=jax.ShapeDtypeStruct((batch * seq, W_B), BF16),
        scratch_shapes=[pltpu.VMEM((ng, BAND_ROWS + seq, HEAD_LANES), BF16),
                        pltpu.VMEM((ng, BAND_ROWS + seq, HEAD_LANES), BF16),
                        pltpu.VMEM((ng, bq, BAND_WIN), F32)],
        compiler_params=_params("parallel", "parallel", "arbitrary"),
        name="attn_b_prompt",
    )(qb, k16, v16, ruler)


def _attn_b_sample_kernel(q_ref, kc_ref, vc_ref, kn_ref, vn_ref, ruler_ref, o_ref, bias_ref):
    n = q_ref.shape[0]
    past = kc_ref.shape[0] // H_B
    scale = DH_B ** -0.5

    @pl.when(pl.program_id(0) == 0)
    def _():
        for h in range(H_B):
            bias_ref[h] = _toeplitz_bias(ruler_ref[h], n, past + n)

    scores = []
    for h in range(H_B):
        q = q_ref[:, h * HEAD_LANES:(h + 1) * HEAD_LANES]
        scores.append((_qk(q, kc_ref[pl.ds(h, past, stride=H_B), :].astype(BF16)), _qk(q, kn_ref[h])))
    for h in range(H_B):
        sc = scores[h][0] * scale + bias_ref[h, :, :past]
        sn = scores[h][1] * scale + bias_ref[h, :, past:]
        acc, l = _softmax_two_parts(sc, sn, vc_ref[pl.ds(h, past, stride=H_B), :].astype(BF16), vn_ref[h])
        o_ref[:, h * HEAD_LANES:(h + 1) * HEAD_LANES] = (acc / l).astype(o_ref.dtype)


def _attn_b_sample(qb, cache_k, cache_v, k16, v16, ruler, layer):
    _, nb, cache_rows, _ = cache_k.shape
    past = cache_rows // H_B
    n = qb.shape[0] // nb
    rows = pl.BlockSpec((n, W_B), lambda b: (b, 0))
    new = pl.BlockSpec((H_B, n, HEAD_LANES), lambda b: (0, b, 0))
    old = pl.BlockSpec((None, None, cache_rows, HEAD_LANES), lambda b: (layer, b, 0, 0))
    return pl.pallas_call(
        _attn_b_sample_kernel,
        grid=(nb,),
        in_specs=[rows, old, old, new, new,
                  pl.BlockSpec((H_B, 1, BIAS_RULER), lambda b: (0, 0, 0))],
        out_specs=rows,
        out_shape=jax.ShapeDtypeStruct(qb.shape, BF16),
        scratch_shapes=[pltpu.VMEM((H_B, n, past + n), F32)],
        compiler_params=_params("arbitrary"),
        name="attn_b_sample",
    )(qb, cache_k, cache_v, k16, v16, ruler)


def _merge_kernel(x_ref, oa_ref, ob_ref, sga_ref, sgb_ref, wa_ref, wb_ref, wo_ref, o_ref):
    a = jnp.dot(oa_ref[...], wa_ref[...], preferred_element_type=F32)
    b = jnp.dot(ob_ref[...], wb_ref[...], preferred_element_type=F32)
    merged = sga_ref[...].astype(F32) * a + sgb_ref[...].astype(F32) * b
    o_ref[...] = x_ref[...] + jnp.dot(merged.astype(BF16), wo_ref[...], preferred_element_type=F32)


def _merge(x, oa, ob, sga, sgb, wa, wb, wo):
    m = x.shape[0]
    bm = MERGE_BM
    wide = pl.BlockSpec((bm, D_MODEL), lambda i: (i, 0))
    half = pl.BlockSpec((bm, W_A), lambda i: (i, 0))

    def whole(shape):
        return pl.BlockSpec(shape, lambda i: (0, 0), pipeline_mode=pl.Buffered(1))

    return pl.pallas_call(
        _merge_kernel,
        grid=(m // bm,),
        in_specs=[wide, half, half, wide, wide,
                  whole((W_A, D_MODEL)), whole((W_B, D_MODEL)), whole((D_MODEL, D_MODEL))],
        out_specs=wide,
        out_shape=jax.ShapeDtypeStruct((m, D_MODEL), F32),
        compiler_params=_params("parallel"),
        name="merge",
    )(x, oa, ob, sga, sgb, wa, wb, wo)


def _rope_tables(pos):
    half = DK_A // 2
    inv = ROPE_THETA ** (-jnp.arange(half, dtype=F32) / half)
    ang = pos.astype(F32)[:, None] * inv[None, :]
    cos, sin = jnp.cos(ang), jnp.sin(ang)
    reps = HEAD_LANES // DK_A
    cos_t = jnp.tile(jnp.concatenate([cos, cos], axis=1), (1, reps))
    sin_t = jnp.tile(jnp.concatenate([-sin, sin], axis=1), (1, reps))
    return cos_t, sin_t


def _bias_ruler(table):
    nh = table.shape[0]
    far_past = jnp.broadcast_to(table[:, 2 * MAX_REL:], (nh, BAND_ROWS - MAX_REL))
    far_future = jnp.broadcast_to(table[:, :1], (nh, BIAS_RULER - ATT_BQ - BAND_ROWS - MAX_REL - 1))
    wrapped = jnp.broadcast_to(table[:, 2 * MAX_REL:], (nh, ATT_BQ))
    ruler = jnp.concatenate([far_past, table[:, ::-1], far_future, wrapped], axis=1)
    return ruler.astype(F32).reshape(nh, 1, BIAS_RULER)


def kernel(x_prompt, x_sample, cache_a_k, cache_a_v, cache_b_k, cache_b_v, ffn1_norm, ffn1_w_gate, ffn1_w_up, ffn1_w_down, mix_norm, w_in, lambda_q1, lambda_k1, lambda_q2, lambda_k2, subln_a, rel_bias_b, w_branch_a, w_branch_b, w_out, ffn2_norm, ffn2_w_gate, ffn2_w_up, ffn2_w_down, final_norm):
    batch, seq, _ = x_prompt.shape
    dec_batch, dec_seq, _ = x_sample.shape
    depth = ffn1_norm.shape[0]
    past_a = cache_a_k.shape[2]
    past_b = cache_b_k.shape[2]
    assert seq % max(ATT_BQ, PROJ_BM) == 0 and seq >= BAND_ROWS
    assert dec_seq == CHUNK and past_b == BAND_ROWS and PROJ_BM % dec_seq == 0

    xp = x_prompt.reshape(batch * seq, D_MODEL)
    xs = x_sample.reshape(dec_batch * dec_seq, D_MODEL)
    rope_p = _rope_tables(jnp.arange(seq))
    rope_s = _rope_tables(past_a + (jnp.arange(PROJ_BM) % dec_seq))
    cache_a_k2 = cache_a_k.reshape(depth, dec_batch, past_a * H_A, HEAD_LANES)
    cache_a_v2 = cache_a_v.reshape(depth, dec_batch, past_a * H_A, HEAD_LANES)
    cache_b_k2 = cache_b_k.reshape(depth, dec_batch, past_b * H_B, HEAD_LANES)
    cache_b_v2 = cache_b_v.reshape(depth, dec_batch, past_b * H_B, HEAD_LANES)

    outs = {k: [] for k in ("akp", "avp", "bkp", "bvp", "aks", "avs", "bks", "bvs")}
    for layer in range(depth):
        lam_init = 0.8 - 0.6 * math.exp(-0.3 * layer)
        lam = (jnp.exp(jnp.sum(lambda_q1[layer].astype(F32) * lambda_k1[layer].astype(F32)))
               - jnp.exp(jnp.sum(lambda_q2[layer].astype(F32) * lambda_k2[layer].astype(F32)))
               + lam_init).reshape(1)
        lam_scale = 1.0 - lam_init
        g1 = ffn1_norm[layer].reshape(1, D_MODEL)
        gm = mix_norm[layer].reshape(1, D_MODEL)
        g2 = ffn2_norm[layer].reshape(1, D_MODEL)
        gf = final_norm.reshape(1, D_MODEL)
        subln = subln_a[layer].reshape(1, DV_A)
        table = rel_bias_b[layer]
        ruler = _bias_ruler(table)
        last = layer == depth - 1

        x1_s, hn_s, w1g, w1u, w1d = _ffn(xs, g1, ffn1_w_gate[layer], ffn1_w_up[layer], ffn1_w_down[layer],
                                         gm, "mix", own_f32=True)
        later = [w[layer] for w in (ffn2_w_gate, ffn2_w_up, ffn2_w_down, w_in, w_branch_a, w_branch_b, w_out)]
        x1_p, hn_p, w2g, w2u, w2d, win, wa, wb, wo = _ffn(xp, g1, w1g, w1u, w1d, gm, "mix", casts=later)

        def trunk(x1, hn, rope_tabs, rope_period, attend, transposed_a):
            c = 0
            qa = _proj(hn, win, c, W_QK_A, "rope_q", rope_tabs, rope_period, transposed=transposed_a); c += W_QK_A
            ka, ka16 = _proj(hn, win, c, W_QK_A, "rope_kv", rope_tabs, rope_period); c += W_QK_A
            va, va16 = _proj(hn, win, c, W_A, "kv", transposed=transposed_a); c += W_A
            qb = _proj(hn, win, c, W_B, "plain"); c += W_B
            kb, kb16 = _proj(hn, win, c, W_B, "kv"); c += W_B
            vb, vb16 = _proj(hn, win, c, W_B, "kv"); c += W_B
            sga = _proj(hn, win, c, D_MODEL, "sigmoid"); c += D_MODEL
            sgb = _proj(hn, win, c, D_MODEL, "sigmoid")
            oa, ob = attend(qa, ka16, va16, qb, kb16, vb16)
            x2 = _merge(x1, oa, ob, sga, sgb, wa, wb, wo)
            xo = _ffn(x2, g2, w2g, w2u, w2d, gf, "final" if last else "mix")[0]
            return xo, ka, va, kb, vb

        def attend_prompt(qa, ka16, va16, qb, kb16, vb16):
            oa = _attn_a_prompt(lam, qa, ka16, va16, subln, batch, seq, lam_scale)
            ob = _attn_b_prompt(qb, kb16, vb16, ruler, batch, seq)
            return oa, ob

        def attend_sample(qa, ka16, va16, qb, kb16, vb16):
            oa = _attn_a_sample(lam, qa, cache_a_k2, cache_a_v2, ka16, va16, subln, layer, lam_scale)
            ob = _attn_b_sample(qb, cache_b_k2, cache_b_v2, kb16, vb16, ruler, layer)
            return oa, ob

        xp, ka, va, kb, vb = trunk(x1_p, hn_p, rope_p, seq, attend_prompt, True)
        outs["akp"].append(ka.reshape(batch, seq, H_A, 2 * DK_A))
        outs["avp"].append(va.reshape(batch, seq, H_A, DV_A))
        rows = min(BAND_ROWS, seq)
        outs["bkp"].append(kb.reshape(batch, seq, H_B, DH_B)[:, seq - rows:])
        outs["bvp"].append(vb.reshape(batch, seq, H_B, DH_B)[:, seq - rows:])

        xs, ka, va, kb, vb = trunk(x1_s, hn_s, rope_s, PROJ_BM, attend_sample, False)
        outs["aks"].append(ka.reshape(dec_batch, dec_seq, H_A, 2 * DK_A))
        outs["avs"].append(va.reshape(dec_batch, dec_seq, H_A, DV_A))
        kb_all = jnp.concatenate([cache_b_k[layer], kb.reshape(dec_batch, dec_seq, H_B, DH_B)], axis=1)
        vb_all = jnp.concatenate([cache_b_v[layer], vb.reshape(dec_batch, dec_seq, H_B, DH_B)], axis=1)
        outs["bks"].append(kb_all[:, dec_seq:])
        outs["bvs"].append(vb_all[:, dec_seq:])

    y_prompt = xp.reshape(batch, seq, D_MODEL)
    y_sample = xs.reshape(dec_batch, dec_seq, D_MODEL)
    return (y_prompt, y_sample,
            jnp.stack(outs["akp"]), jnp.stack(outs["avp"]), jnp.stack(outs["bkp"]), jnp.stack(outs["bvp"]),
            jnp.stack(outs["aks"]), jnp.stack(outs["avs"]), jnp.stack(outs["bks"]), jnp.stack(outs["bvs"]))
```

```python
import functools
import math

import jax
import jax.numpy as jnp
from jax import lax
from jax.experimental import pallas as pl
from jax.experimental.pallas import tpu as pltpu

F32 = jnp.float32
BF16 = jnp.bfloat16

D_MODEL = 2048
CHUNK = 64
H_A = 8
DK_A = 64
DV_A = 2 * DK_A
W_QK_A = H_A * 2 * DK_A
W_A = H_A * DV_A
H_B = 8
DH_B = 128
W_B = H_B * DH_B
N_PREV = 8
BAND_ROWS = N_PREV * CHUNK
MAX_REL = 128
FFN_DIM = 4 * D_MODEL
ROPE_THETA = 10000.0
EPS = 1e-6
NEG = -1e30

HEAD_LANES = 128
BF16_SUBLANES = 16
VMEM_LIMIT = 58 * 1024 * 1024

FFN_BM = 512
FFN_BF = 1024
FFN_OWN_BM = 1024
FFN_OWN_BF = 256
PROJ_BM = 1024
PROJ_BN = 1024
MERGE_BM = 512
ATT_BQ = 256
BAND_HEADS_PER_STEP = 4
BAND_WIN = BAND_ROWS + ATT_BQ
BIAS_RULER = BAND_WIN + ATT_BQ


def _params(*sem):
    return pltpu.CompilerParams(dimension_semantics=sem, vmem_limit_bytes=VMEM_LIMIT)


def _rms(x, g):
    return x * lax.rsqrt(jnp.mean(x * x, axis=-1, keepdims=True) + EPS) * g


def _ffn_kernel(x_ref, g_ref, wg_ref, wu_ref, wd_ref, g2_ref, *rest, mode, ncast, own_f32, roll_layer):
    cast_in, rest = rest[:ncast], rest[ncast:]
    if roll_layer is not None:
        roll_in, rest = rest[:4], rest[4:]
    if mode == "mix":
        o_ref, hn_ref, *rest = rest
    else:
        o_ref, *rest = rest
    cast_out, rest = rest[:ncast], rest[ncast:]
    if own_f32:
        wg16_ref, wu16_ref, wd16_ref, *rest = rest
    if roll_layer is not None:
        roll_out, rest = rest[:2], rest[2:]
    h_ref, *rest = rest
    f = pl.program_id(1)

    if roll_layer is not None:
        (sem,) = rest
        ck_ref, cv_ref, nk_ref, nv_ref = roll_in
        new_rows = nk_ref.shape[1]
        keep = ck_ref.shape[2] - new_rows

        def roll_copies():
            copies = []
            for n, (c_ref, n_ref, r_ref) in enumerate(((ck_ref, nk_ref, roll_out[0]), (cv_ref, nv_ref, roll_out[1]))):
                copies.append(pltpu.make_async_copy(c_ref.at[roll_layer, :, pl.ds(new_rows, keep), :],
                                                    r_ref.at[:, pl.ds(0, keep), :], sem.at[2 * n]))
                copies.append(pltpu.make_async_copy(n_ref, r_ref.at[:, pl.ds(keep, new_rows), :], sem.at[2 * n + 1]))
            return copies

        first = jnp.logical_and(pl.program_id(0) == 0, f == 0)
        final = jnp.logical_and(pl.program_id(0) == pl.num_programs(0) - 1, f == pl.num_programs(1) - 1)

        @pl.when(first)
        def _():
            for c in roll_copies():
                c.start()

    for src, dst in zip(cast_in, cast_out):
        dst[...] = src[...].astype(BF16)

    @pl.when(f == 0)
    def _():
        x = x_ref[...]
        h_ref[...] = _rms(x, g_ref[...]).astype(BF16)
        o_ref[...] = x

    def weight(w_ref, w16_ref):
        if not own_f32:
            return w_ref[...]
        w16_ref[...] = w_ref[...].astype(BF16)
        return w16_ref[...]

    h = h_ref[...]
    a = jnp.dot(h, weight(wg_ref, wg16_ref if own_f32 else None), preferred_element_type=F32)
    b = jnp.dot(h, weight(wu_ref, wu16_ref if own_f32 else None), preferred_element_type=F32)
    t = (0.5 * (a * jax.nn.sigmoid(a)) * b).astype(BF16)
    o_ref[...] += jnp.dot(t, weight(wd_ref, wd16_ref if own_f32 else None), preferred_element_type=F32)

    @pl.when(f == pl.num_programs(1) - 1)
    def _():
        y = _rms(o_ref[...], g2_ref[...])
        if mode == "mix":
            hn_ref[...] = y.astype(BF16)
        else:
            o_ref[...] = y

    if roll_layer is not None:
        @pl.when(final)
        def _():
            for c in roll_copies():
                c.wait()


def _ffn(x, g, wg, wu, wd, g2, mode, casts=(), own_f32=False, roll=None):
    m = x.shape[0]
    bm, bf = (FFN_OWN_BM, FFN_OWN_BF) if own_f32 else (FFN_BM, FFN_BF)
    grid = (m // bm, FFN_DIM // bf)
    nf = grid[1]
    row = pl.BlockSpec((bm, D_MODEL), lambda i, f: (i, 0), pipeline_mode=pl.Buffered(1) if own_f32 else None)
    vec = pl.BlockSpec((1, D_MODEL), lambda i, f: (0, 0))
    w_specs = [
        pl.BlockSpec((D_MODEL, bf), lambda i, f: (0, f)),
        pl.BlockSpec((D_MODEL, bf), lambda i, f: (0, f)),
        pl.BlockSpec((bf, D_MODEL), lambda i, f: (f, 0)),
    ]
    in_specs = [row, vec, *w_specs, vec]
    if mode == "mix":
        out_shape = [jax.ShapeDtypeStruct((m, D_MODEL), F32), jax.ShapeDtypeStruct((m, D_MODEL), BF16)]
        out_specs = [row, row]
    else:
        out_shape = [jax.ShapeDtypeStruct((m, D_MODEL), F32)]
        out_specs = [row]
    nsteps = grid[0] * nf
    for w in casts:
        rows, cols = w.shape
        if rows % (nsteps * BF16_SUBLANES) == 0:
            block_rows, hold = rows // nsteps, 1
        else:
            block_rows, hold = BF16_SUBLANES, nsteps * BF16_SUBLANES // rows
            assert rows * hold == nsteps * BF16_SUBLANES
        spec = pl.BlockSpec((block_rows, cols), functools.partial(lambda i, f, hold: ((i * nf + f) // hold, 0), hold=hold))
        in_specs.append(spec)
        out_specs.append(spec)
        out_shape.append(jax.ShapeDtypeStruct(w.shape, BF16))
    if own_f32:
        def once(f, i):
            return jnp.where(i == 0, f, nf)
        out_specs += [pl.BlockSpec((D_MODEL, bf), lambda i, f: (0, once(f, i))),
                      pl.BlockSpec((D_MODEL, bf), lambda i, f: (0, once(f, i))),
                      pl.BlockSpec((bf, D_MODEL), lambda i, f: (once(f, i), 0))]
        spare = bf
        out_shape += [jax.ShapeDtypeStruct((D_MODEL, FFN_DIM + spare), BF16),
                      jax.ShapeDtypeStruct((D_MODEL, FFN_DIM + spare), BF16),
                      jax.ShapeDtypeStruct((FFN_DIM + spare, D_MODEL), BF16)]
    scratch = [pltpu.VMEM((bm, D_MODEL), BF16)]
    args = [x, g, wg, wu, wd, g2, *casts]
    roll_layer = None
    if roll is not None:
        *roll_arrays, roll_layer = roll
        in_hbm = pl.BlockSpec(memory_space=pl.ANY)
        in_specs += [in_hbm] * 4
        args += roll_arrays
        out_specs += [in_hbm] * 2
        out_shape += [jax.ShapeDtypeStruct(roll_arrays[0].shape[1:], F32)] * 2
        scratch.append(pltpu.SemaphoreType.DMA((4,)))
    sequential = own_f32 or roll is not None
    return pl.pallas_call(
        functools.partial(_ffn_kernel, mode=mode, ncast=len(casts), own_f32=own_f32, roll_layer=roll_layer),
        grid=grid, in_specs=in_specs, out_specs=out_specs, out_shape=out_shape,
        scratch_shapes=scratch,
        compiler_params=_params("arbitrary" if sequential else "parallel", "arbitrary"),
        name="ffn_" + mode + ("_f32w" if own_f32 else "") + ("_roll" if roll is not None else ""),
    )(*args)


def _rope_slabs(z, cos, sin_signed):
    lane = lax.broadcasted_iota(jnp.int32, cos.shape, 1)
    first_half = (lane % DK_A) < (DK_A // 2)
    outs = []
    for s in range(z.shape[1] // HEAD_LANES):
        slab = z[:, s * HEAD_LANES:(s + 1) * HEAD_LANES]
        partner = jnp.where(first_half,
                            pltpu.roll(slab, HEAD_LANES - DK_A // 2, 1),
                            pltpu.roll(slab, DK_A // 2, 1))
        outs.append(slab * cos + partner * sin_signed)
    return jnp.concatenate(outs, axis=1)


def _proj_kernel(h_ref, w_ref, *rest, epilogue, transposed):
    rope = epilogue in ("rope_q", "rope_kv")
    heads = epilogue in ("rope_kv", "kv")
    if rope:
        cos_ref, sin_ref, *rest = rest
    z = jnp.dot(h_ref[...], w_ref[...], preferred_element_type=F32)
    if rope:
        z = _rope_slabs(z, cos_ref[...], sin_ref[...])
    if epilogue == "rope_q":
        z = z * (DK_A ** -0.5)
    elif epilogue == "sigmoid":
        z = jax.nn.sigmoid(z)
    nh = z.shape[1] // HEAD_LANES
    if heads:
        o_ref, o16_ref = rest
        for h in range(nh):
            zh = z[:, h * HEAD_LANES:(h + 1) * HEAD_LANES]
            o_ref[pl.ds(h, z.shape[0], stride=nh), :] = zh
            o16_ref[h] = (zh.T if transposed else zh).astype(BF16)
    elif transposed:
        (o_ref,) = rest
        for h in range(nh):
            o_ref[h] = z[:, h * HEAD_LANES:(h + 1) * HEAD_LANES].T.astype(BF16)
    else:
        (o_ref,) = rest
        o_ref[...] = z.astype(o_ref.dtype)


def _proj(hn, w_in, col0, ncols, epilogue, rope_tabs=None, rope_period=None, transposed=False):
    m = hn.shape[0]
    bm, bn = PROJ_BM, PROJ_BN
    joff = col0 // bn
    nh = bn // HEAD_LANES
    head_major = (pl.BlockSpec((nh, HEAD_LANES, bm), lambda i, j: (0, 0, i)) if transposed
                  else pl.BlockSpec((nh, bm, HEAD_LANES), lambda i, j: (0, i, 0)))
    head_major_shape = jax.ShapeDtypeStruct((nh, HEAD_LANES, m) if transposed else (nh, m, HEAD_LANES), BF16)
    in_specs = [
        pl.BlockSpec((bm, D_MODEL), lambda i, j: (i, 0)),
        pl.BlockSpec((D_MODEL, bn), lambda i, j: (0, j + joff)),
    ]
    args = [hn, w_in]
    if rope_tabs is not None:
        nper = rope_period // bm
        tab = pl.BlockSpec((bm, HEAD_LANES), lambda i, j: (i % nper, 0))
        in_specs += [tab, tab]
        args += list(rope_tabs)
    if epilogue in ("rope_kv", "kv"):
        assert ncols == bn
        out_specs = (pl.BlockSpec((bm * nh, HEAD_LANES), lambda i, j: (i, 0)), head_major)
        out_shape = (jax.ShapeDtypeStruct((m * nh, HEAD_LANES), F32), head_major_shape)
    elif transposed:
        assert ncols == bn
        out_specs, out_shape = head_major, head_major_shape
    else:
        out_specs = pl.BlockSpec((bm, bn), lambda i, j: (i, j))
        out_shape = jax.ShapeDtypeStruct((m, ncols), BF16)
    return pl.pallas_call(
        functools.partial(_proj_kernel, epilogue=epilogue, transposed=transposed),
        grid=(m // bm, ncols // bn), in_specs=in_specs,
        out_specs=out_specs, out_shape=out_shape,
        compiler_params=_params("parallel", "arbitrary"),
        name="proj_" + epilogue,
    )(*args)


def _block_diag_q(q):
    lane = lax.broadcasted_iota(jnp.int32, q.shape, 1)
    zero = jnp.zeros_like(q)
    return jnp.concatenate([jnp.where(lane < DK_A, q, zero), jnp.where(lane >= DK_A, q, zero)], axis=0)


def _qk(q, k):
    return lax.dot_general(q, k, (((1,), (1,)), ((), ())), preferred_element_type=F32)


def _attn_a_prompt_kernel(lam_ref, qt_ref, k_ref, vt_ref, subln_ref, o_ref, *, lam_scale):
    qi = pl.program_id(1)
    nh, _, bq = qt_ref.shape
    row = lax.broadcasted_iota(jnp.int32, (HEAD_LANES, bq), 0)
    qbd = []
    for h in range(nh):
        qt = qt_ref[h]
        zero = jnp.zeros_like(qt)
        qbd.append(jnp.concatenate([jnp.where(row < DK_A, qt, zero), jnp.where(row >= DK_A, qt, zero)], axis=1))

    def step(start, bk, carries, masked):
        scores = [jnp.dot(k_ref[h, pl.ds(start, bk), :], qbd[h], preferred_element_type=F32) for h in range(nh)]
        if masked:
            kr = lax.broadcasted_iota(jnp.int32, (bq, 2 * bq), 0)
            qc = lax.broadcasted_iota(jnp.int32, (bq, 2 * bq), 1) % bq
            visible = (kr // CHUNK) <= (qc // CHUNK)
        out = []
        for h in range(nh):
            m, l, acc = carries[h]
            s = scores[h]
            if masked:
                s = jnp.where(visible, s, NEG)
            m_new = jnp.maximum(m, jnp.max(s, axis=0, keepdims=True))
            alpha = jnp.exp(m - m_new)
            p = jnp.exp(s - m_new)
            l = alpha * l + jnp.sum(p, axis=0, keepdims=True)
            acc = alpha * acc + jnp.dot(vt_ref[h, :, pl.ds(start, bk)], p.astype(BF16),
                                        preferred_element_type=F32)
            out.append((m_new, l, acc))
        return tuple(out)

    init = (jnp.full((1, 2 * bq), -jnp.inf, F32), jnp.zeros((1, 2 * bq), F32),
            jnp.zeros((HEAD_LANES, 2 * bq), F32))
    wide = 2 * bq
    carries = lax.fori_loop(
        0, qi // 2, lambda j, c: step(pl.multiple_of(j * wide, wide), wide, c, masked=False), (init,) * nh)
    carries = lax.fori_loop(
        0, qi % 2, lambda _, c: step(pl.multiple_of((qi - 1) * bq, bq), bq, c, masked=False), carries)
    carries = step(pl.multiple_of(qi * bq, bq), bq, carries, masked=True)
    for h in range(nh):
        _, l, acc = carries[h]
        o = acc / l
        o = (o[:, :bq] - lam_ref[0] * o[:, bq:]).T
        o = _rms(o, subln_ref[...]) * lam_scale
        o_ref[:, h * HEAD_LANES:(h + 1) * HEAD_LANES] = o.astype(o_ref.dtype)


def _attn_a_prompt(lam, qt, k16, vt, subln, batch, seq, lam_scale):
    bq = ATT_BQ
    nq = seq // bq
    return pl.pallas_call(
        functools.partial(_attn_a_prompt_kernel, lam_scale=lam_scale),
        grid=(batch, nq),
        in_specs=[
            pl.BlockSpec(memory_space=pltpu.SMEM),
            pl.BlockSpec((H_A, HEAD_LANES, bq), lambda b, i: (0, 0, b * nq + i)),
            pl.BlockSpec((H_A, seq, HEAD_LANES), lambda b, i: (0, b, 0)),
            pl.BlockSpec((H_A, HEAD_LANES, seq), lambda b, i: (0, 0, b)),
            pl.BlockSpec((1, DV_A), lambda b, i: (0, 0)),
        ],
        out_specs=pl.BlockSpec((bq, W_A), lambda b, i: (b * nq + i, 0)),
        out_shape=jax.ShapeDtypeStruct((batch * seq, W_A), BF16),
        compiler_params=_params("parallel", "arbitrary"),
        name="attn_a_prompt",
    )(lam, qt, k16, vt, subln)


def _softmax_two_parts(sc, sn, vc, vn):
    m = jnp.maximum(jnp.max(sc, axis=-1, keepdims=True), jnp.max(sn, axis=-1, keepdims=True))
    pc = jnp.exp(sc - m)
    pn = jnp.exp(sn - m)
    l = jnp.sum(pc, axis=-1, keepdims=True) + jnp.sum(pn, axis=-1, keepdims=True)
    acc = (jnp.dot(pc.astype(BF16), vc, preferred_element_type=F32)
           + jnp.dot(pn.astype(BF16), vn, preferred_element_type=F32))
    return acc, l


def _tn(a, b):
    return lax.dot_general(a, b, (((0,), (0,)), ((), ())), preferred_element_type=F32)


def _attn_a_sample_kernel(lam_ref, q_ref, kc_ref, vc_ref, kn_ref, vn_ref, subln_ref, o_ref, *, lam_scale):
    n = q_ref.shape[0]
    past = kc_ref.shape[0] // H_A
    scores = []
    for h in range(H_A):
        qbd_t = _block_diag_q(q_ref[:, h * HEAD_LANES:(h + 1) * HEAD_LANES]).astype(F32).T.astype(BF16)
        kc = kc_ref[pl.ds(h, past, stride=H_A), :].astype(BF16)
        scores.append((jnp.dot(kc, qbd_t, preferred_element_type=F32),
                       jnp.dot(kn_ref[h], qbd_t, preferred_element_type=F32)))
    for h in range(H_A):
        sc, sn = scores[h]
        m = jnp.maximum(jnp.max(sc, axis=0, keepdims=True), jnp.max(sn, axis=0, keepdims=True))
        pc = jnp.exp(sc - m)
        pn = jnp.exp(sn - m)
        l = jnp.sum(pc, axis=0, keepdims=True) + jnp.sum(pn, axis=0, keepdims=True)
        acc = (_tn(vc_ref[pl.ds(h, past, stride=H_A), :].astype(BF16), pc.astype(BF16))
               + _tn(vn_ref[h], pn.astype(BF16)))
        o = acc / l
        o = (o[:, :n] - lam_ref[0] * o[:, n:]).T
        o = _rms(o, subln_ref[...]) * lam_scale
        o_ref[:, h * HEAD_LANES:(h + 1) * HEAD_LANES] = o.astype(o_ref.dtype)


def _attn_a_sample(lam, qa, cache_k, cache_v, k16, v16, subln, layer, lam_scale):
    _, nb, cache_rows, _ = cache_k.shape
    n = qa.shape[0] // nb
    rows = pl.BlockSpec((n, W_A), lambda b: (b, 0))
    new = pl.BlockSpec((H_A, n, HEAD_LANES), lambda b: (0, b, 0))
    old = pl.BlockSpec((None, None, cache_rows, HEAD_LANES), lambda b: (layer, b, 0, 0))
    return pl.pallas_call(
        functools.partial(_attn_a_sample_kernel, lam_scale=lam_scale),
        grid=(nb,),
        in_specs=[pl.BlockSpec(memory_space=pltpu.SMEM), rows, old, old, new, new,
                  pl.BlockSpec((1, DV_A), lambda b: (0, 0))],
        out_specs=rows,
        out_shape=jax.ShapeDtypeStruct(qa.shape, BF16),
        compiler_params=_params("parallel"),
        name="attn_a_sample",
    )(lam, qa, cache_k, cache_v, k16, v16, subln)


def _toeplitz_bias(ruler_row, nq, nk):
    rows = jnp.broadcast_to(ruler_row, (nq, ruler_row.shape[1]))
    return pltpu.roll(rows, 0, 1, stride=1, stride_axis=0)[:, :nk]


def _attn_b_prompt_kernel(q_ref, k_ref, v_ref, ruler_ref, o_ref, kb_ref, vb_ref, bias_ref):
    qi = pl.program_id(2)
    bq = q_ref.shape[0]
    ng, seq, _ = k_ref.shape

    @pl.when(qi == 0)
    def _():
        zeros = jnp.zeros((BAND_ROWS, HEAD_LANES), BF16)
        d = (lax.broadcasted_iota(jnp.int32, (bq, BAND_WIN), 1) // CHUNK
             - lax.broadcasted_iota(jnp.int32, (bq, BAND_WIN), 0) // CHUNK)
        for g in range(ng):
            kb_ref[g, 0:BAND_ROWS, :] = zeros
            vb_ref[g, 0:BAND_ROWS, :] = zeros
            kb_ref[g, BAND_ROWS:BAND_ROWS + seq, :] = k_ref[g]
            vb_ref[g, BAND_ROWS:BAND_ROWS + seq, :] = v_ref[g]
            bias = _toeplitz_bias(ruler_ref[g], bq, BAND_WIN)
            bias_ref[g] = jnp.where(d >= 0, jnp.where(d <= N_PREV, bias, NEG), NEG)

    start = pl.multiple_of(qi * bq, bq)
    win = pl.ds(start, BAND_WIN)
    scores = [_qk(q_ref[:, g * HEAD_LANES:(g + 1) * HEAD_LANES], kb_ref[g, win, :]) for g in range(ng)]
    w = lax.broadcasted_iota(jnp.int32, (bq, BAND_WIN), 1)
    in_sequence = w + qi * bq >= BAND_ROWS
    for g in range(ng):
        s = scores[g] * (DH_B ** -0.5) + bias_ref[g]
        s = jnp.where(in_sequence, s, NEG)
        p = jnp.exp(s - jnp.max(s, axis=-1, keepdims=True))
        l = jnp.sum(p, axis=-1, keepdims=True)
        o = jnp.dot(p.astype(BF16), vb_ref[g, win, :], preferred_element_type=F32) / l
        o_ref[:, g * HEAD_LANES:(g + 1) * HEAD_LANES] = o.astype(o_ref.dtype)


def _attn_b_prompt(qb, k16, v16, ruler, batch, seq):
    bq, ng = ATT_BQ, BAND_HEADS_PER_STEP
    nq = seq // bq
    kv = pl.BlockSpec((ng, seq, HEAD_LANES), lambda b, h, i: (h, b, 0))
    rows = pl.BlockSpec((bq, ng * HEAD_LANES), lambda b, h, i: (b * nq + i, h))
    return pl.pallas_call(
        _attn_b_prompt_kernel,
        grid=(batch, H_B // ng, nq),
        in_specs=[rows, kv, kv, pl.BlockSpec((ng, 1, BIAS_RULER), lambda b, h, i: (h, 0, 0))],
        out_specs=rows,
        out_shape=jax.ShapeDtypeStruct((batch * seq, W_B), BF16),
        scratch_shapes=[pltpu.VMEM((ng, BAND_ROWS + seq, HEAD_LANES), BF16),
                        pltpu.VMEM((ng, BAND_ROWS + seq, HEAD_LANES), BF16),
                        pltpu.VMEM((ng, bq, BAND_WIN), F32)],
        compiler_params=_params("parallel", "parallel", "arbitrary"),
        name="attn_b_prompt",
    )(qb, k16, v16, ruler)


def _attn_b_sample_kernel(q_ref, kc_ref, vc_ref, kn_ref, vn_ref, ruler_ref, o_ref, bias_ref):
    n = q_ref.shape[0]
    past = kc_ref.shape[0] // H_B
    scale = DH_B ** -0.5

    @pl.when(pl.program_id(0) == 0)
    def _():
        for h in range(H_B):
            bias_ref[h] = _toeplitz_bias(ruler_ref[h], n, past + n)

    scores = []
    for h in range(H_B):
        q = q_ref[:, h * HEAD_LANES:(h + 1) * HEAD_LANES]
        scores.append((_qk(q, kc_ref[pl.ds(h, past, stride=H_B), :].astype(BF16)), _qk(q, kn_ref[h])))
    for h in range(H_B):
        sc = scores[h][0] * scale + bias_ref[h, :, :past]
        sn = scores[h][1] * scale + bias_ref[h, :, past:]
        acc, l = _softmax_two_parts(sc, sn, vc_ref[pl.ds(h, past, stride=H_B), :].astype(BF16), vn_ref[h])
        o_ref[:, h * HEAD_LANES:(h + 1) * HEAD_LANES] = (acc / l).astype(o_ref.dtype)


def _attn_b_sample(qb, cache_k, cache_v, k16, v16, ruler, layer):
    _, nb, cache_rows, _ = cache_k.shape
    past = cache_rows // H_B
    n = qb.shape[0] // nb
    rows = pl.BlockSpec((n, W_B), lambda b: (b, 0))
    new = pl.BlockSpec((H_B, n, HEAD_LANES), lambda b: (0, b, 0))
    old = pl.BlockSpec((None, None, cache_rows, HEAD_LANES), lambda b: (layer, b, 0, 0))
    return pl.pallas_call(
        _attn_b_sample_kernel,
        grid=(nb,),
        in_specs=[rows, old, old, new, new,
                  pl.BlockSpec((H_B, 1, BIAS_RULER), lambda b: (0, 0, 0))],
        out_specs=rows,
        out_shape=jax.ShapeDtypeStruct(qb.shape, BF16),
        scratch_shapes=[pltpu.VMEM((H_B, n, past + n), F32)],
        compiler_params=_params("arbitrary"),
        name="attn_b_sample",
    )(qb, cache_k, cache_v, k16, v16, ruler)


def _merge_kernel(x_ref, oa_ref, ob_ref, sga_ref, sgb_ref, wa_ref, wb_ref, wo_ref, o_ref):
    a = jnp.dot(oa_ref[...], wa_ref[...], preferred_element_type=F32)
    b = jnp.dot(ob_ref[...], wb_ref[...], preferred_element_type=F32)
    merged = sga_ref[...].astype(F32) * a + sgb_ref[...].astype(F32) * b
    o_ref[...] = x_ref[...] + jnp.dot(merged.astype(BF16), wo_ref[...], preferred_element_type=F32)


def _merge(x, oa, ob, sga, sgb, wa, wb, wo):
    m = x.shape[0]
    bm = MERGE_BM
    wide = pl.BlockSpec((bm, D_MODEL), lambda i: (i, 0))
    half = pl.BlockSpec((bm, W_A), lambda i: (i, 0))

    def whole(shape):
        return pl.BlockSpec(shape, lambda i: (0, 0), pipeline_mode=pl.Buffered(1))

    return pl.pallas_call(
        _merge_kernel,
        grid=(m // bm,),
        in_specs=[wide, half, half, wide, wide,
                  whole((W_A, D_MODEL)), whole((W_B, D_MODEL)), whole((D_MODEL, D_MODEL))],
        out_specs=wide,
        out_shape=jax.ShapeDtypeStruct((m, D_MODEL), F32),
        compiler_params=_params("parallel"),
        name="merge",
    )(x, oa, ob, sga, sgb, wa, wb, wo)


def _rope_tables(pos):
    half = DK_A // 2
    inv = ROPE_THETA ** (-jnp.arange(half, dtype=F32) / half)
    ang = pos.astype(F32)[:, None] * inv[None, :]
    cos, sin = jnp.cos(ang), jnp.sin(ang)
    reps = HEAD_LANES // DK_A
    cos_t = jnp.tile(jnp.concatenate([cos, cos], axis=1), (1, reps))
    sin_t = jnp.tile(jnp.concatenate([-sin, sin], axis=1), (1, reps))
    return cos_t, sin_t


def _bias_ruler(table):
    nh = table.shape[0]
    far_past = jnp.broadcast_to(table[:, 2 * MAX_REL:], (nh, BAND_ROWS - MAX_REL))
    far_future = jnp.broadcast_to(table[:, :1], (nh, BIAS_RULER - ATT_BQ - BAND_ROWS - MAX_REL - 1))
    wrapped = jnp.broadcast_to(table[:, 2 * MAX_REL:], (nh, ATT_BQ))
    ruler = jnp.concatenate([far_past, table[:, ::-1], far_future, wrapped], axis=1)
    return ruler.astype(F32).reshape(nh, 1, BIAS_RULER)


def kernel(x_prompt, x_sample, cache_a_k, cache_a_v, cache_b_k, cache_b_v, ffn1_norm, ffn1_w_gate, ffn1_w_up, ffn1_w_down, mix_norm, w_in, lambda_q1, lambda_k1, lambda_q2, lambda_k2, subln_a, rel_bias_b, w_branch_a, w_branch_b, w_out, ffn2_norm, ffn2_w_gate, ffn2_w_up, ffn2_w_down, final_norm):
    batch, seq, _ = x_prompt.shape
    dec_batch, dec_seq, _ = x_sample.shape
    depth = ffn1_norm.shape[0]
    past_a = cache_a_k.shape[2]
    past_b = cache_b_k.shape[2]
    assert seq % max(ATT_BQ, PROJ_BM) == 0 and seq >= BAND_ROWS
    assert dec_seq == CHUNK and past_b == BAND_ROWS and PROJ_BM % dec_seq == 0

    xp = x_prompt.reshape(batch * seq, D_MODEL)
    xs = x_sample.reshape(dec_batch * dec_seq, D_MODEL)
    rope_p = _rope_tables(jnp.arange(seq))
    rope_s = _rope_tables(past_a + (jnp.arange(PROJ_BM) % dec_seq))
    cache_a_k2 = cache_a_k.reshape(depth, dec_batch, past_a * H_A, HEAD_LANES)
    cache_a_v2 = cache_a_v.reshape(depth, dec_batch, past_a * H_A, HEAD_LANES)
    cache_b_k2 = cache_b_k.reshape(depth, dec_batch, past_b * H_B, HEAD_LANES)
    cache_b_v2 = cache_b_v.reshape(depth, dec_batch, past_b * H_B, HEAD_LANES)

    outs = {k: [] for k in ("akp", "avp", "bkp", "bvp", "aks", "avs", "bks", "bvs")}
    for layer in range(depth):
        lam_init = 0.8 - 0.6 * math.exp(-0.3 * layer)
        lam = (jnp.exp(jnp.sum(lambda_q1[layer].astype(F32) * lambda_k1[layer].astype(F32)))
               - jnp.exp(jnp.sum(lambda_q2[layer].astype(F32) * lambda_k2[layer].astype(F32)))
               + lam_init).reshape(1)
        lam_scale = 1.0 - lam_init
        g1 = ffn1_norm[layer].reshape(1, D_MODEL)
        gm = mix_norm[layer].reshape(1, D_MODEL)
        g2 = ffn2_norm[layer].reshape(1, D_MODEL)
        gf = final_norm.reshape(1, D_MODEL)
        subln = subln_a[layer].reshape(1, DV_A)
        table = rel_bias_b[layer]
        ruler = _bias_ruler(table)
        last = layer == depth - 1

        x1_s, hn_s, w1g, w1u, w1d = _ffn(xs, g1, ffn1_w_gate[layer], ffn1_w_up[layer], ffn1_w_down[layer],
                                         gm, "mix", own_f32=True)
        later = [w[layer] for w in (ffn2_w_gate, ffn2_w_up, ffn2_w_down, w_in, w_branch_a, w_branch_b, w_out)]
        x1_p, hn_p, w2g, w2u, w2d, win, wa, wb, wo = _ffn(xp, g1, w1g, w1u, w1d, gm, "mix", casts=later)

        def trunk(x1, hn, rope_tabs, rope_period, attend, transposed_a):
            c = 0
            qa = _proj(hn, win, c, W_QK_A, "rope_q", rope_tabs, rope_period, transposed=transposed_a); c += W_QK_A
            ka, ka16 = _proj(hn, win, c, W_QK_A, "rope_kv", rope_tabs, rope_period); c += W_QK_A
            va, va16 = _proj(hn, win, c, W_A, "kv", transposed=transposed_a); c += W_A
            qb = _proj(hn, win, c, W_B, "plain"); c += W_B
            kb, kb16 = _proj(hn, win, c, W_B, "kv"); c += W_B
            vb, vb16 = _proj(hn, win, c, W_B, "kv"); c += W_B
            sga = _proj(hn, win, c, D_MODEL, "sigmoid"); c += D_MODEL
            sgb = _proj(hn, win, c, D_MODEL, "sigmoid")
            oa, ob = attend(qa, ka16, va16, qb, kb16, vb16)
            x2 = _merge(x1, oa, ob, sga, sgb, wa, wb, wo)
            return x2, ka, va, kb, vb

        def attend_prompt(qa, ka16, va16, qb, kb16, vb16):
            oa = _attn_a_prompt(lam, qa, ka16, va16, subln, batch, seq, lam_scale)
            ob = _attn_b_prompt(qb, kb16, vb16, ruler, batch, seq)
            return oa, ob

        def attend_sample(qa, ka16, va16, qb, kb16, vb16):
            oa = _attn_a_sample(lam, qa, cache_a_k2, cache_a_v2, ka16, va16, subln, layer, lam_scale)
            ob = _attn_b_sample(qb, cache_b_k2, cache_b_v2, kb16, vb16, ruler, layer)
            return oa, ob

        x2_p, ka, va, kb, vb = trunk(x1_p, hn_p, rope_p, seq, attend_prompt, True)
        outs["akp"].append(ka.reshape(batch, seq, H_A, 2 * DK_A))
        outs["avp"].append(va.reshape(batch, seq, H_A, DV_A))
        rows = min(BAND_ROWS, seq)
        outs["bkp"].append(kb.reshape(batch, seq, H_B, DH_B)[:, seq - rows:])
        outs["bvp"].append(vb.reshape(batch, seq, H_B, DH_B)[:, seq - rows:])

        x2_s, ka, va, kb, vb = trunk(x1_s, hn_s, rope_s, PROJ_BM, attend_sample, False)
        outs["aks"].append(ka.reshape(dec_batch, dec_seq, H_A, 2 * DK_A))
        outs["avs"].append(va.reshape(dec_batch, dec_seq, H_A, DV_A))

        new_rows = dec_seq * H_B
        roll = (cache_b_k2, cache_b_v2, kb.reshape(dec_batch, new_rows, HEAD_LANES),
                vb.reshape(dec_batch, new_rows, HEAD_LANES), layer)
        ffn2_mode = "final" if last else "mix"
        xp, *_, bk_rolled, bv_rolled = _ffn(x2_p, g2, w2g, w2u, w2d, gf, ffn2_mode, roll=roll)
        xs = _ffn(x2_s, g2, w2g, w2u, w2d, gf, ffn2_mode)[0]
        outs["bks"].append(bk_rolled.reshape(dec_batch, past_b, H_B, DH_B))
        outs["bvs"].append(bv_rolled.reshape(dec_batch, past_b, H_B, DH_B))

    y_prompt = xp.reshape(batch, seq, D_MODEL)
    y_sample = xs.reshape(dec_batch, dec_seq, D_MODEL)
    return (y_prompt, y_sample,
            jnp.stack(outs["akp"]), jnp.stack(outs["avp"]), jnp.stack(outs["bkp"]), jnp.stack(outs["bvp"]),
            jnp.stack(outs["aks"]), jnp.stack(outs["avs"]), jnp.stack(outs["bks"]), jnp.stack(outs["bvs"]))
```
